```python
import math
import jax, jax.numpy as jnp
from jax import lax
import numpy as np

D_MODEL = 1024
BATCH = 16
SEQ = 4096
DEPTH = 4

N_A_LAYERS = DEPTH // 2
N_B_LAYERS = DEPTH - N_A_LAYERS

A_HEADS = 8
A_DK = 128
A_DV = 128
A_QK = A_HEADS * A_DK
A_V = A_HEADS * A_DV
A_CONV = 4
A_CONV_CH = 2 * A_QK + A_V
A_CHUNK = 64
A_IN = 2 * A_QK + 2 * A_V + 2 * A_HEADS

B_HEADS = 16
B_DH = 64
B_W = B_HEADS * B_DH
B_BLOCK = 128
B_IN = 2 * B_W
KV_IN = 2 * B_W + B_HEADS

NORM_EPS = 1e-6

kernel_name = "yoco_gdn_fox_sandwich_adaln"


def rms_norm(x, gain):
    xf = x.astype(jnp.float32)
    y = xf * lax.rsqrt(jnp.mean(xf * xf, axis=-1, keepdims=True) + NORM_EPS)
    return (y * gain.astype(jnp.float32)).astype(x.dtype)


def l2_norm(x):
    xf = x.astype(jnp.float32)
    return xf * lax.rsqrt(jnp.sum(xf * xf, axis=-1, keepdims=True) + NORM_EPS)


def adaln(c, w, b, n):
    m = jax.nn.silu(c) @ w + b
    return [t[:, None, :] for t in jnp.split(m, n, axis=-1)]


def causal_dwconv_silu(x, w):
    k_w, ch = w.shape
    y = lax.conv_general_dilated(
        x, w[:, None, :].astype(x.dtype), window_strides=(1,), padding=[(k_w - 1, 0)],
        dimension_numbers=('NWC', 'WIO', 'NWC'), feature_group_count=ch)
    return jax.nn.silu(y)


def gated_delta_rule(q, k, v, g, beta):
    bsz, seq, heads, dk = q.shape
    dv = v.shape[-1]
    n_chunks = seq // A_CHUNK
    f32 = jnp.float32

    def to_chunks(t):
        t = t.astype(f32).reshape((bsz, n_chunks, A_CHUNK, heads) + t.shape[3:])
        return jnp.moveaxis(t, (1, 3), (0, 2))

    q_c = to_chunks(q) * (dk ** -0.5)
    k_c = to_chunks(k)
    v_c = to_chunks(v)
    b_c = to_chunks(beta)
    g_c = jnp.cumsum(to_chunks(g), axis=-1)
    tri = jnp.tril(jnp.ones((A_CHUNK, A_CHUNK), dtype=bool))
    decay = jnp.exp(jnp.where(tri, g_c[..., :, None] - g_c[..., None, :], -jnp.inf))

    k_beta = k_c * b_c[..., None]
    a_strict = jnp.tril(jnp.einsum('nbhid,nbhjd->nbhij', k_beta, k_c) * decay, -1)
    rhs = jnp.concatenate([v_c * b_c[..., None], k_beta * jnp.exp(g_c)[..., None]], axis=-1)
    sol = lax.linalg.triangular_solve(a_strict + jnp.eye(A_CHUNK, dtype=f32), rhs,
                                      left_side=True, lower=True, unit_diagonal=True)
    u_c, w_c = sol[..., :dv], sol[..., dv:]
    qk = jnp.einsum('nbhid,nbhjd->nbhij', q_c, k_c) * decay

    def step(state, inp):
        q_i, k_i, u_i, w_i, g_i, qk_i = inp
        v_new = u_i - jnp.einsum('bhck,bhkv->bhcv', w_i, state)
        o_i = (jnp.einsum('bhck,bhkv->bhcv', q_i * jnp.exp(g_i)[..., None], state)
               + jnp.einsum('bhij,bhjv->bhiv', qk_i, v_new))
        g_last = g_i[..., -1:]
        state = (state * jnp.exp(g_last)[..., None]
                 + jnp.einsum('bhck,bhcv->bhkv', k_i * jnp.exp(g_last - g_i)[..., None], v_new))
        return state, o_i

    s0 = jnp.zeros((bsz, heads, dk, dv), f32)
    _, o = lax.scan(step, s0, (q_c, k_c, u_c, w_c, g_c, qk))
    return jnp.moveaxis(o, (0, 2), (1, 3)).reshape(bsz, seq, heads, dv)


def gated_deltanet(h, w_in, conv_w, a_log, dt_bias, o_gain, w_out):
    bsz, seq, _ = h.shape
    proj = h @ w_in
    qkv, z, a, b = jnp.split(proj, [A_CONV_CH, A_CONV_CH + A_V, A_CONV_CH + A_V + A_HEADS], axis=-1)
    qkv = causal_dwconv_silu(qkv, conv_w)
    q, k, v = jnp.split(qkv, [A_QK, 2 * A_QK], axis=-1)
    q = l2_norm(q.reshape(bsz, seq, A_HEADS, A_DK))
    k = l2_norm(k.reshape(bsz, seq, A_HEADS, A_DK))
    v = v.reshape(bsz, seq, A_HEADS, A_DV)
    beta = jax.nn.sigmoid(b.astype(jnp.float32))
    g = -jnp.exp(a_log.astype(jnp.float32)) * jax.nn.softplus(a.astype(jnp.float32) + dt_bias.astype(jnp.float32))
    o = gated_delta_rule(q, k, v, g, beta)
    o = rms_norm(o, o_gain).astype(h.dtype).reshape(bsz, seq, A_V)
    return (o * jax.nn.silu(z)) @ w_out


def shared_kv(x, c, mod_w, mod_b, norm_g, w_kvf, b_f):
    bsz, seq, _ = x.shape
    shift, scale = adaln(c, mod_w, mod_b, 2)
    h = rms_norm(x, norm_g) * (1 + scale) + shift
    k, v, f = jnp.split(h @ w_kvf, [B_W, 2 * B_W], axis=-1)
    k = k.reshape(bsz, seq, B_HEADS, B_DH).transpose(0, 2, 1, 3)
    v = v.reshape(bsz, seq, B_HEADS, B_DH).transpose(0, 2, 1, 3)
    log_f = jax.nn.log_sigmoid(f.astype(jnp.float32) + b_f.astype(jnp.float32))
    log_f_cum = jnp.cumsum(log_f, axis=1).transpose(0, 2, 1)
    return k, v, log_f_cum


def forgetting_attention(h, w_in, w_out, k, v, log_f_cum):
    bsz, seq, _ = h.shape
    q, z = jnp.split(h @ w_in, 2, axis=-1)
    q = q.reshape(bsz, seq, B_HEADS, B_DH).transpose(0, 2, 1, 3) * (B_DH ** -0.5)
    outs = []
    for blk in range(seq // B_BLOCK):
        s0 = blk * B_BLOCK
        s1 = s0 + B_BLOCK
        logits = (jnp.einsum('bhqd,bhkd->bhqk', q[:, :, s0:s1], k[:, :, :s1]).astype(jnp.float32)
                  + log_f_cum[:, :, s0:s1, None] - log_f_cum[:, :, None, :s1])
        causal = jnp.arange(s0, s1)[:, None] >= jnp.arange(s1)[None, :]
        p = jax.nn.softmax(jnp.where(causal, logits, -jnp.inf), axis=-1)
        outs.append(jnp.einsum('bhqk,bhkd->bhqd', p.astype(v.dtype), v[:, :, :s1]))
    o = jnp.concatenate(outs, axis=2).transpose(0, 2, 1, 3).reshape(bsz, seq, B_W)
    return (o * jax.nn.silu(z)) @ w_out


def setup_inputs(seed: int = 0) -> dict:
    key = jax.random.key(seed)
    ks = iter(jax.random.split(key, 40))
    D = D_MODEL
    na, nb = N_A_LAYERS, N_B_LAYERS

    def nrm(shape, scale):
        return scale * jax.random.normal(next(ks), shape, jnp.float32)

    x = nrm((BATCH, SEQ, D), 1.0)
    c = nrm((BATCH, D), 1.0)
    a_mod_w = nrm((na, D, 3 * D), 0.5 * D ** -0.5)
    a_mod_b = nrm((na, 3 * D), 0.02)
    a_pre_g = 1.0 + nrm((na, D), 0.05)
    a_post_g = 1.0 + nrm((na, D), 0.05)
    a_w_in = nrm((na, D, A_IN), D ** -0.5)
    a_conv_w = nrm((na, A_CONV, A_CONV_CH), A_CONV ** -0.5)
    a_a_log = jnp.log(jax.random.uniform(next(ks), (na, A_HEADS), jnp.float32, 1.0, 16.0))
    dt = jnp.exp(jax.random.uniform(next(ks), (na, A_HEADS), jnp.float32, math.log(1e-3), math.log(0.1)))
    a_dt_bias = dt + jnp.log(-jnp.expm1(-dt))
    a_o_gain = 1.0 + nrm((na, A_DV), 0.05)
    a_w_out = nrm((na, A_V, D), A_V ** -0.5)
    kv_mod_w = nrm((D, 2 * D), 0.5 * D ** -0.5)
    kv_mod_b = nrm((2 * D,), 0.02)
    kv_norm_g = 1.0 + nrm((D,), 0.05)
    kv_w = jnp.concatenate([nrm((D, 2 * B_W), D ** -0.5), nrm((D, B_HEADS), 0.1 * D ** -0.5)], axis=1)
    kv_b_f = jax.random.uniform(next(ks), (B_HEADS,), jnp.float32, 1.0, 6.0)
    b_mod_w = nrm((nb, D, 3 * D), 0.5 * D ** -0.5)
    b_mod_b = nrm((nb, 3 * D), 0.02)
    b_pre_g = 1.0 + nrm((nb, D), 0.05)
    b_post_g = 1.0 + nrm((nb, D), 0.05)
    b_w_in = nrm((nb, D, B_IN), D ** -0.5)
    b_w_out = nrm((nb, B_W, D), B_W ** -0.5)
    return {"x": x, "c": c,
            "a_mod_w": a_mod_w, "a_mod_b": a_mod_b, "a_pre_g": a_pre_g, "a_post_g": a_post_g,
            "a_w_in": a_w_in, "a_conv_w": a_conv_w, "a_a_log": a_a_log, "a_dt_bias": a_dt_bias,
            "a_o_gain": a_o_gain, "a_w_out": a_w_out,
            "kv_mod_w": kv_mod_w, "kv_mod_b": kv_mod_b, "kv_norm_g": kv_norm_g, "kv_w": kv_w,
            "kv_b_f": kv_b_f,
            "b_mod_w": b_mod_w, "b_mod_b": b_mod_b, "b_pre_g": b_pre_g, "b_post_g": b_post_g,
            "b_w_in": b_w_in, "b_w_out": b_w_out}


def reference(x, c, a_mod_w, a_mod_b, a_pre_g, a_post_g, a_w_in, a_conv_w, a_a_log, a_dt_bias,
              a_o_gain, a_w_out, kv_mod_w, kv_mod_b, kv_norm_g, kv_w, kv_b_f,
              b_mod_w, b_mod_b, b_pre_g, b_post_g, b_w_in, b_w_out):
    k_sh = v_sh = f_cum = None
    for layer in range(DEPTH):
        if layer < N_A_LAYERS:
            j = layer
            shift, scale, gate = adaln(c, a_mod_w[j], a_mod_b[j], 3)
            h = rms_norm(x, a_pre_g[j]) * (1 + scale) + shift
            y = gated_deltanet(h, a_w_in[j], a_conv_w[j], a_a_log[j], a_dt_bias[j], a_o_gain[j], a_w_out[j])
            x = x + gate * rms_norm(y, a_post_g[j])
            if layer == N_A_LAYERS - 1:
                k_sh, v_sh, f_cum = shared_kv(x, c, kv_mod_w, kv_mod_b, kv_norm_g, kv_w, kv_b_f)
        else:
            j = layer - N_A_LAYERS
            shift, scale, gate = adaln(c, b_mod_w[j], b_mod_b[j], 3)
            h = rms_norm(x, b_pre_g[j]) * (1 + scale) + shift
            y = forgetting_attention(h, b_w_in[j], b_w_out[j], k_sh, v_sh, f_cum)
            x = x + gate * rms_norm(y, b_post_g[j])
    return x
```

```python
import functools

import jax
import jax.numpy as jnp
import numpy as np
from jax import lax
from jax.experimental import pallas as pl
from jax.experimental.pallas import tpu as pltpu

F32 = jnp.float32
BF16 = jnp.bfloat16

D_MODEL = 1024
N_A_LAYERS = 2
N_B_LAYERS = 2
A_HEADS = 8
A_DK = 128
A_DV = 128
A_QK = A_HEADS * A_DK
A_V = A_HEADS * A_DV
A_CONV = 4
A_CONV_CH = 2 * A_QK + A_V
B_HEADS = 16
B_DH = 64
B_W = B_HEADS * B_DH
NORM_EPS = 1e-6

LANES = 128
SUBLANES = 8
GDN_CHUNK = 128
VMEM_LIMIT = 56 * 1024 * 1024

AUG_W = 128
AUG_F0 = B_DH
N_FPARTS = 3


def _cparams(sem):
    return pltpu.CompilerParams(dimension_semantics=sem, vmem_limit_bytes=VMEM_LIMIT)


def _silu(x):
    return x / (1.0 + jnp.exp(-x))


def _softplus(x):
    return jnp.maximum(x, 0.0) + jnp.log1p(jnp.exp(-jnp.abs(x)))


def _dot(a, b):
    return jnp.dot(a.astype(BF16), b.astype(BF16), preferred_element_type=F32)


def _dot_nt(a, b):
    return lax.dot_general(a.astype(BF16), b.astype(BF16), (((1,), (1,)), ((), ())),
                           preferred_element_type=F32)


def _dot_tn(a, b):
    return lax.dot_general(a.astype(BF16), b.astype(BF16), (((0,), (0,)), ((), ())),
                           preferred_element_type=F32)


def _split2(a):
    hi = a.astype(BF16)
    lo = (a - hi.astype(F32)).astype(BF16)
    return hi, lo


def _dot3(a, b):
    ah, al = _split2(a)
    bh, bl = _split2(b)
    mm = functools.partial(jnp.dot, preferred_element_type=F32)
    return mm(ah, bh) + (mm(ah, bl) + mm(al, bh))


def _norm_mod(x, gain, scale, shift):
    ms = jnp.mean(x * x, axis=-1, keepdims=True)
    y = x * lax.rsqrt(ms + NORM_EPS) * gain
    return y * (1.0 + scale) + shift


def _mod_kernel(c_ref, w_ref, b_ref, o_ref):
    s = _silu(c_ref[...])
    o_ref[...] = jnp.dot(s, w_ref[...], precision=lax.Precision.HIGHEST,
                         preferred_element_type=F32) + b_ref[...]


def _adaln(c, w, b):
    bsz, d = c.shape
    n = w.shape[1]
    tn = 512
    return pl.pallas_call(
        _mod_kernel,
        grid=(n // tn,),
        in_specs=[pl.BlockSpec((bsz, d), lambda j: (0, 0)),
                  pl.BlockSpec((d, tn), lambda j: (0, j)),
                  pl.BlockSpec((1, tn), lambda j: (0, j))],
        out_specs=pl.BlockSpec((bsz, tn), lambda j: (0, j)),
        out_shape=jax.ShapeDtypeStruct((bsz, n), F32),
        compiler_params=_cparams(("arbitrary",)),
        name="adaln_mod",
    )(c, w, b.reshape(1, n))


A_IN_TM = 512
A_IN_TN = 512


def _a_in_kernel(x_ref, shift_ref, scale_ref, g_ref, w_ref, wab_ref, qkv_ref, z_ref, ab_ref):
    h = _norm_mod(x_ref[0], g_ref[...], scale_ref[0], shift_ref[0]).astype(BF16)
    for j in range(A_CONV_CH // A_IN_TN):
        sl = slice(j * A_IN_TN, (j + 1) * A_IN_TN)
        qkv_ref[0, :, sl] = jnp.dot(h, w_ref[:, sl], preferred_element_type=F32).astype(BF16)
    for j in range(A_V // A_IN_TN):
        sl = slice(j * A_IN_TN, (j + 1) * A_IN_TN)
        wsl = slice(A_CONV_CH + j * A_IN_TN, A_CONV_CH + (j + 1) * A_IN_TN)
        z_ref[0, :, sl] = jnp.dot(h, w_ref[:, wsl], preferred_element_type=F32).astype(BF16)
    ab_ref[0] = jnp.dot(h, wab_ref[...], preferred_element_type=F32)


def _a_in_proj(x, shift, scale, gain, w_main, w_ab):
    bsz, seq, d = x.shape
    tm = A_IN_TM
    row = lambda b, i: (b, i, 0)
    vec = lambda b, i: (b, 0, 0)
    const = lambda b, i: (0, 0)
    return pl.pallas_call(
        _a_in_kernel,
        grid=(bsz, seq // tm),
        in_specs=[pl.BlockSpec((1, tm, d), row),
                  pl.BlockSpec((1, 1, d), vec),
                  pl.BlockSpec((1, 1, d), vec),
                  pl.BlockSpec((1, d), const),
                  pl.BlockSpec(w_main.shape, const),
                  pl.BlockSpec(w_ab.shape, const)],
        out_specs=[pl.BlockSpec((1, tm, A_CONV_CH), row),
                   pl.BlockSpec((1, tm, A_V), row),
                   pl.BlockSpec((1, tm, LANES), row)],
        out_shape=[jax.ShapeDtypeStruct((bsz, seq, A_CONV_CH), BF16),
                   jax.ShapeDtypeStruct((bsz, seq, A_V), BF16),
                   jax.ShapeDtypeStruct((bsz, seq, LANES), F32)],
        compiler_params=_cparams(("parallel", "parallel")),
        name="a_in_proj",
    )(x, shift, scale, gain, w_main, w_ab)


def _inv_unit_lower(a, row, col):
    c = a.shape[0]
    eye = jnp.where(row == col, 1.0, 0.0).astype(F32)

    def blk(shift):
        return (row >> shift) == (col >> shift)

    b_prev = blk(3)
    ad = jnp.where(b_prev, a, 0.0)
    a2 = _dot3(ad, ad)
    a3 = _dot3(ad, a2)
    a4 = _dot3(a2, a2)
    p1 = eye - ad + a2 - a3
    t = p1 + _dot3(p1, a4)
    shift = 4
    while (1 << shift) <= c:
        b_cur = blk(shift)
        a_off = jnp.where(jnp.logical_and(b_cur, jnp.logical_not(b_prev)), a, 0.0)
        t = t - _dot3(t, _dot3(a_off, t))
        b_prev = b_cur
        shift += 1
    return t


def _gdn_kernel(qkv_ref, z_ref, ab_ref, convw_ref, alog_ref, dtb_ref, ogain_ref, o_ref,
                xe_ref, state_ref):
    c = GDN_CHUNK
    t_idx = pl.program_id(1)

    @pl.when(t_idx == 0)
    def _():
        xe_ref[0:SUBLANES, :] = jnp.zeros((SUBLANES, A_CONV_CH), F32)
        state_ref[...] = jnp.zeros_like(state_ref)

    x = qkv_ref[0].astype(F32)
    xe_ref[SUBLANES:SUBLANES + c, :] = x
    y = convw_ref[A_CONV - 1:A_CONV, :] * x
    for k in range(A_CONV - 1):
        off = SUBLANES - (A_CONV - 1) + k
        y = y + convw_ref[k:k + 1, :] * xe_ref[off:off + c, :]
    xe_ref[0:SUBLANES, :] = x[c - SUBLANES:c, :]
    y = _silu(y)

    ab = ab_ref[0]
    g = -jnp.exp(alog_ref[...]) * _softplus(ab + dtb_ref[...])
    beta = 1.0 / (1.0 + jnp.exp(-ab))
    row = lax.broadcasted_iota(jnp.int32, (c, c), 0)
    col = lax.broadcasted_iota(jnp.int32, (c, c), 1)
    tril = jnp.where(row >= col, 1.0, 0.0).astype(F32)
    gc = jnp.dot(tril, g, precision=lax.Precision.HIGHEST, preferred_element_type=F32)
    gc_t = gc.T
    eg = jnp.exp(gc)
    g_last = gc[c - 1:c, :]
    e_last = jnp.exp(g_last)
    e_rest = jnp.exp(g_last - gc)
    lower = row >= col
    strict = row > col

    for h in range(A_HEADS):
        sl = slice(h * A_DK, (h + 1) * A_DK)
        q = y[:, sl]
        k = y[:, A_QK + h * A_DK:A_QK + (h + 1) * A_DK]
        v = y[:, 2 * A_QK + h * A_DV:2 * A_QK + (h + 1) * A_DV]
        q = q * (lax.rsqrt(jnp.sum(q * q, axis=-1, keepdims=True) + NORM_EPS) * (A_DK ** -0.5))
        k = k * lax.rsqrt(jnp.sum(k * k, axis=-1, keepdims=True) + NORM_EPS)
        b_col = beta[:, A_HEADS + h:A_HEADS + h + 1]
        diff = gc[:, h:h + 1] - gc_t[h:h + 1, :]
        dec = jnp.exp(jnp.where(lower, diff, -jnp.inf))
        eg_col = eg[:, h:h + 1]
        kb = k * b_col
        m = _dot_nt(jnp.concatenate([kb, q], axis=0), k)
        a = jnp.where(strict, m[:c] * dec, 0.0)
        qk = m[c:] * dec
        t_inv = _inv_unit_lower(a, row, col)
        rhs = jnp.concatenate([v * b_col, kb * eg_col], axis=1)
        sol = _dot3(t_inv, rhs)
        u = sol[:, :A_DV]
        w = sol[:, A_DV:]
        s_old = state_ref[h]
        wq = _dot(jnp.concatenate([w, q * eg_col], axis=0), s_old)
        v_new = u - wq[:c]
        o = wq[c:] + _dot(qk, v_new)
        kd = k * e_rest[:, h:h + 1]
        state_ref[h] = s_old * e_last[:, h:h + 1] + _dot_tn(kd, v_new)
        o = o * lax.rsqrt(jnp.mean(o * o, axis=-1, keepdims=True) + NORM_EPS) * ogain_ref[...]
        zz = z_ref[0, :, sl].astype(F32)
        o_ref[0, :, sl] = (o * _silu(zz)).astype(BF16)


def _gdn_mixer(qkv, z, ab, conv_w, a_log_pad, dt_bias_pad, o_gain):
    bsz, seq, _ = qkv.shape
    c = GDN_CHUNK
    row = lambda b, i: (b, i, 0)
    const = lambda b, i: (0, 0)
    return pl.pallas_call(
        _gdn_kernel,
        grid=(bsz, seq // c),
        in_specs=[pl.BlockSpec((1, c, A_CONV_CH), row),
                  pl.BlockSpec((1, c, A_V), row),
                  pl.BlockSpec((1, c, LANES), row),
                  pl.BlockSpec((A_CONV, A_CONV_CH), const),
                  pl.BlockSpec((1, LANES), const),
                  pl.BlockSpec((1, LANES), const),
                  pl.BlockSpec((1, A_DV), const)],
        out_specs=pl.BlockSpec((1, c, A_V), row),
        out_shape=jax.ShapeDtypeStruct((bsz, seq, A_V), BF16),
        scratch_shapes=[pltpu.VMEM((SUBLANES + c, A_CONV_CH), F32),
                        pltpu.VMEM((A_HEADS, A_DK, A_DV), F32)],
        compiler_params=_cparams(("parallel", "arbitrary")),
        name="gdn_mixer",
    )(qkv, z, ab, conv_w, a_log_pad, dt_bias_pad, o_gain)


OUT_TM = 512


def _out_kernel(og_ref, x_ref, gate_ref, g_ref, w_ref, o_ref):
    y = jnp.dot(og_ref[0], w_ref[...], preferred_element_type=F32)
    yn = y * lax.rsqrt(jnp.mean(y * y, axis=-1, keepdims=True) + NORM_EPS) * g_ref[...]
    o_ref[0] = x_ref[0] + gate_ref[0] * yn


def _out_proj(og, x, gate, gain, w):
    bsz, seq, d = x.shape
    tm = OUT_TM
    row = lambda b, i: (b, i, 0)
    vec = lambda b, i: (b, 0, 0)
    const = lambda b, i: (0, 0)
    return pl.pallas_call(
        _out_kernel,
        grid=(bsz, seq // tm),
        in_specs=[pl.BlockSpec((1, tm, og.shape[-1]), row),
                  pl.BlockSpec((1, tm, d), row),
                  pl.BlockSpec((1, 1, d), vec),
                  pl.BlockSpec((1, d), const),
                  pl.BlockSpec(w.shape, const)],
        out_specs=pl.BlockSpec((1, tm, d), row),
        out_shape=jax.ShapeDtypeStruct((bsz, seq, d), F32),
        compiler_params=_cparams(("parallel", "parallel")),
        name="out_proj",
    )(og, x, gate, gain, w)


KV_TM = 512


def _fparts(f):
    hi = f.astype(BF16).astype(F32)
    r1 = f - hi
    mid = r1.astype(BF16).astype(F32)
    lo = (r1 - mid).astype(BF16).astype(F32)
    lane = lax.broadcasted_iota(jnp.int32, f.shape, 1)
    out = jnp.where(lane < B_HEADS, hi, 0.0)
    out = out + jnp.where(jnp.logical_and(lane >= B_HEADS, lane < 2 * B_HEADS),
                          pltpu.roll(mid, B_HEADS, 1), 0.0)
    out = out + jnp.where(jnp.logical_and(lane >= 2 * B_HEADS, lane < 3 * B_HEADS),
                          pltpu.roll(lo, 2 * B_HEADS, 1), 0.0)
    return out


def _kv_kernel(x_ref, shift_ref, scale_ref, g_ref, wk_ref, wv_ref, wf_ref, bf_ref, place_ref,
               ones_ref, k_ref, v_ref, fcat_ref, carry_ref):
    tm = x_ref.shape[1]

    @pl.when(pl.program_id(1) == 0)
    def _():
        carry_ref[...] = jnp.zeros_like(carry_ref)

    h = _norm_mod(x_ref[0], g_ref[...], scale_ref[0], shift_ref[0]).astype(BF16)
    v_ref[0] = jnp.dot(h, wv_ref[...], preferred_element_type=F32).astype(BF16)
    f = jnp.dot(h, wf_ref[...], preferred_element_type=F32) + bf_ref[...]
    lane = lax.broadcasted_iota(jnp.int32, f.shape, 1)
    log_f = jnp.where(lane < B_HEADS, -_softplus(-f), 0.0)
    row = lax.broadcasted_iota(jnp.int32, (tm, tm), 0)
    col = lax.broadcasted_iota(jnp.int32, (tm, tm), 1)
    tril = jnp.where(row >= col, 1.0, 0.0).astype(F32)
    f_cum = jnp.dot(tril, log_f, precision=lax.Precision.HIGHEST,
                    preferred_element_type=F32) + carry_ref[...]
    carry_ref[...] = f_cum[tm - 1:tm, :]
    fcat = _fparts(f_cum).astype(BF16)
    fcat_ref[0] = fcat
    for j in range(k_ref.shape[2] // 512):
        sl = slice(j * 512, (j + 1) * 512)
        k_aug = (jnp.dot(h, wk_ref[:, sl], preferred_element_type=F32)
                 + jnp.dot(fcat, place_ref[:, sl], preferred_element_type=F32) + ones_ref[:, sl])
        k_ref[0, :, sl] = k_aug.astype(BF16)


def _kv_proj(x, shift, scale, gain, wk_aug, wv, wf, bf, place_k, ones_k):
    bsz, seq, d = x.shape
    tm = KV_TM
    row = lambda b, i: (b, i, 0)
    vec = lambda b, i: (b, 0, 0)
    const = lambda b, i: (0, 0)
    kw = B_HEADS * AUG_W
    return pl.pallas_call(
        _kv_kernel,
        grid=(bsz, seq // tm),
        in_specs=[pl.BlockSpec((1, tm, d), row),
                  pl.BlockSpec((1, 1, d), vec),
                  pl.BlockSpec((1, 1, d), vec),
                  pl.BlockSpec((1, d), const),
                  pl.BlockSpec(wk_aug.shape, const),
                  pl.BlockSpec(wv.shape, const),
                  pl.BlockSpec(wf.shape, const),
                  pl.BlockSpec((1, LANES), const),
                  pl.BlockSpec(place_k.shape, const),
                  pl.BlockSpec((1, kw), const)],
        out_specs=[pl.BlockSpec((1, tm, kw), row),
                   pl.BlockSpec((1, tm, B_W), row),
                   pl.BlockSpec((1, tm, LANES), row)],
        out_shape=[jax.ShapeDtypeStruct((bsz, seq, kw), BF16),
                   jax.ShapeDtypeStruct((bsz, seq, B_W), BF16),
                   jax.ShapeDtypeStruct((bsz, seq, LANES), BF16)],
        scratch_shapes=[pltpu.VMEM((1, LANES), F32)],
        compiler_params=_cparams(("parallel", "arbitrary")),
        name="kv_proj",
    )(x, shift, scale, gain, wk_aug, wv, wf, bf, place_k, ones_k)


B_IN_TM = 512


def _b_in_kernel(x_ref, shift_ref, scale_ref, g_ref, wq_ref, wz_ref, fcat_ref, place_ref, ones_ref,
                 q_ref, z_ref):
    h = _norm_mod(x_ref[0], g_ref[...], scale_ref[0], shift_ref[0]).astype(BF16)
    fcat = fcat_ref[0]
    for j in range(q_ref.shape[2] // 512):
        sl = slice(j * 512, (j + 1) * 512)
        q_aug = (jnp.dot(h, wq_ref[:, sl], preferred_element_type=F32)
                 + jnp.dot(fcat, place_ref[:, sl], preferred_element_type=F32) + ones_ref[:, sl])
        q_ref[0, :, sl] = q_aug.astype(BF16)
    for j in range(z_ref.shape[2] // 512):
        sl = slice(j * 512, (j + 1) * 512)
        z_ref[0, :, sl] = jnp.dot(h, wz_ref[:, sl], preferred_element_type=F32).astype(BF16)


def _b_in_proj(x, shift, scale, gain, wq_aug, wz, fcat, place_q, ones_q):
    bsz, seq, d = x.shape
    tm = B_IN_TM
    row = lambda b, i: (b, i, 0)
    vec = lambda b, i: (b, 0, 0)
    const = lambda b, i: (0, 0)
    qw = B_HEADS * AUG_W
    return pl.pallas_call(
        _b_in_kernel,
        grid=(bsz, seq // tm),
        in_specs=[pl.BlockSpec((1, tm, d), row),
                  pl.BlockSpec((1, 1, d), vec),
                  pl.BlockSpec((1, 1, d), vec),
                  pl.BlockSpec((1, d), const),
                  pl.BlockSpec(wq_aug.shape, const),
                  pl.BlockSpec(wz.shape, const),
                  pl.BlockSpec((1, tm, LANES), row),
                  pl.BlockSpec(place_q.shape, const),
                  pl.BlockSpec((1, qw), const)],
        out_specs=[pl.BlockSpec((1, tm, qw), row),
                   pl.BlockSpec((1, tm, B_W), row)],
        out_shape=[jax.ShapeDtypeStruct((bsz, seq, qw), BF16),
                   jax.ShapeDtypeStruct((bsz, seq, B_W), BF16)],
        compiler_params=_cparams(("parallel", "parallel")),
        name="b_in_proj",
    )(x, shift, scale, gain, wq_aug, wz, fcat, place_q, ones_q)


ATT_T = 256
HEADS_PER_STEP = 2


def _attn_kernel(q_ref, k_ref, v_ref, z_ref, o_ref):
    t = ATT_T
    i = pl.program_id(2)
    qs = [q_ref[0, :, hh * AUG_W:(hh + 1) * AUG_W] for hh in range(HEADS_PER_STEP)]

    def step(j, carry, masked):
        start = pl.multiple_of(j * t, t)
        vv = v_ref[0, pl.ds(start, t), :]
        new = []
        for hh in range(HEADS_PER_STEP):
            m, l, acc = carry[hh]
            kk = k_ref[0, pl.ds(start, t), hh * AUG_W:(hh + 1) * AUG_W]
            s = lax.dot_general(qs[hh], kk, (((1,), (1,)), ((), ())), preferred_element_type=F32)
            if masked:
                row = lax.broadcasted_iota(jnp.int32, (t, t), 0)
                col = lax.broadcasted_iota(jnp.int32, (t, t), 1)
                s = jnp.where(row >= col, s, -jnp.inf)
            m_new = jnp.maximum(m, jnp.max(s, axis=-1, keepdims=True))
            p = jnp.exp(s - m_new)
            alpha = jnp.exp(m - m_new)
            l = alpha * l + jnp.sum(p, axis=-1, keepdims=True)
            acc = alpha * acc + jnp.dot(p.astype(BF16), vv, preferred_element_type=F32)
            new.append((m_new, l, acc))
        return tuple(new)

    init = tuple((jnp.full((t, 1), -jnp.inf, F32), jnp.zeros((t, 1), F32),
                  jnp.zeros((t, LANES), F32)) for _ in range(HEADS_PER_STEP))
    carry = lax.fori_loop(0, i, lambda j, cr: step(j, cr, False), init)
    carry = step(i, carry, True)
    outs = [acc / l for (_, l, acc) in carry]
    lane = lax.broadcasted_iota(jnp.int32, (t, LANES), 1)
    o = jnp.where(lane < B_DH, outs[0], outs[1])
    o_ref[0] = (o * _silu(z_ref[0].astype(F32))).astype(BF16)


def _fox_attention(q_aug, k_aug, v, z):
    bsz, seq, _ = v.shape
    t = ATT_T
    n_pairs = B_HEADS // HEADS_PER_STEP
    pw = HEADS_PER_STEP * AUG_W
    return pl.pallas_call(
        _attn_kernel,
        grid=(bsz, n_pairs, seq // t),
        in_specs=[pl.BlockSpec((1, t, pw), lambda b, p, i: (b, i, p)),
                  pl.BlockSpec((1, seq, pw), lambda b, p, i: (b, 0, p)),
                  pl.BlockSpec((1, seq, LANES), lambda b, p, i: (b, 0, p)),
                  pl.BlockSpec((1, t, LANES), lambda b, p, i: (b, i, p))],
        out_specs=pl.BlockSpec((1, t, LANES), lambda b, p, i: (b, i, p)),
        out_shape=jax.ShapeDtypeStruct((bsz, seq, B_W), BF16),
        compiler_params=_cparams(("parallel", "parallel", "arbitrary")),
        name="fox_attention",
    )(q_aug, k_aug, v, z)


def _pad_cols(w, n):
    return jnp.pad(w, ((0, 0), (0, n - w.shape[1])))


def _aug_weight(w, scale):
    d = w.shape[0]
    w = (w * scale).reshape(d, B_HEADS, B_DH)
    w = jnp.pad(w, ((0, 0), (0, 0), (0, AUG_W - B_DH)))
    return w.reshape(d, B_HEADS * AUG_W).astype(BF16)


def _placement(f_off, one_off, sign):
    place = np.zeros((LANES, B_HEADS * AUG_W), np.float32)
    ones = np.zeros((1, B_HEADS * AUG_W), np.float32)
    for p in range(N_FPARTS):
        for h in range(B_HEADS):
            place[p * B_HEADS + h, h * AUG_W + f_off + p] = sign
            ones[0, h * AUG_W + one_off + p] = 1.0
    return jnp.asarray(place, BF16), jnp.asarray(ones, F32)


def _vec3(m, n):
    return [t[:, None, :] for t in jnp.split(m, n, axis=-1)]


def kernel(x, c, a_mod_w, a_mod_b, a_pre_g, a_post_g, a_w_in, a_conv_w, a_a_log, a_dt_bias, a_o_gain, a_w_out, kv_mod_w, kv_mod_b, kv_norm_g, kv_w, kv_b_f, b_mod_w, b_mod_b, b_pre_g, b_post_g, b_w_in, b_w_out):
    place_q, ones_q = _placement(AUG_F0, AUG_F0 + N_FPARTS, 1.0)
    place_k, ones_k = _placement(AUG_F0 + N_FPARTS, AUG_F0, -1.0)

    for j in range(N_A_LAYERS):
        shift, scale, gate = _vec3(_adaln(c, a_mod_w[j], a_mod_b[j]), 3)
        w_in = a_w_in[j]
        n_main = A_CONV_CH + A_V
        w_main = w_in[:, :n_main].astype(BF16)
        w_ab = _pad_cols(w_in[:, n_main:], LANES).astype(BF16)
        qkv, z, ab = _a_in_proj(x, shift, scale, a_pre_g[j][None, :], w_main, w_ab)
        a_log_pad = _pad_cols(a_a_log[j][None, :], LANES)
        dt_pad = _pad_cols(a_dt_bias[j][None, :], LANES)
        og = _gdn_mixer(qkv, z, ab, a_conv_w[j], a_log_pad, dt_pad, a_o_gain[j][None, :])
        x = _out_proj(og, x, gate, a_post_g[j][None, :], a_w_out[j].astype(BF16))

    shift, scale = _vec3(_adaln(c, kv_mod_w, kv_mod_b), 2)
    wk_aug = _aug_weight(kv_w[:, :B_W], 1.0)
    wv = kv_w[:, B_W:2 * B_W].astype(BF16)
    wf = _pad_cols(kv_w[:, 2 * B_W:], LANES).astype(BF16)
    bf = _pad_cols(kv_b_f[None, :], LANES)
    k_aug, v, fcat = _kv_proj(x, shift, scale, kv_norm_g[None, :], wk_aug, wv, wf, bf, place_k, ones_k)

    for j in range(N_B_LAYERS):
        shift, scale, gate = _vec3(_adaln(c, b_mod_w[j], b_mod_b[j]), 3)
        wq_aug = _aug_weight(b_w_in[j][:, :B_W], B_DH ** -0.5)
        wz = b_w_in[j][:, B_W:].astype(BF16)
        q_aug, z = _b_in_proj(x, shift, scale, b_pre_g[j][None, :], wq_aug, wz, fcat, place_q, ones_q)
        og = _fox_attention(q_aug, k_aug, v, z)
        x = _out_proj(og, x, gate, b_post_g[j][None, :], b_w_out[j].astype(BF16))
    return x
```

```python
import functools

import jax
import jax.numpy as jnp
import numpy as np
from jax import lax
from jax.experimental import pallas as pl
from jax.experimental.pallas import tpu as pltpu

F32 = jnp.float32
BF16 = jnp.bfloat16

D_MODEL = 1024
N_A_LAYERS = 2
N_B_LAYERS = 2
A_HEADS = 8
A_DK = 128
A_DV = 128
A_QK = A_HEADS * A_DK
A_V = A_HEADS * A_DV
A_CONV = 4
A_CONV_CH = 2 * A_QK + A_V
B_HEADS = 16
B_DH = 64
B_W = B_HEADS * B_DH
NORM_EPS = 1e-6

LANES = 128
SUBLANES = 8
GDN_CHUNK = 128
VMEM_LIMIT = 56 * 1024 * 1024

AUG_W = 128
AUG_F0 = B_DH
N_FPARTS = 3


def _cparams(sem):
    return pltpu.CompilerParams(dimension_semantics=sem, vmem_limit_bytes=VMEM_LIMIT)


def _silu(x):
    return x / (1.0 + jnp.exp(-x))


def _softplus(x):
    return jnp.maximum(x, 0.0) + jnp.log1p(jnp.exp(-jnp.abs(x)))


def _dot(a, b):
    return jnp.dot(a.astype(BF16), b.astype(BF16), preferred_element_type=F32)


def _dot_nt(a, b):
    return lax.dot_general(a.astype(BF16), b.astype(BF16), (((1,), (1,)), ((), ())),
                           preferred_element_type=F32)


def _dot_tn(a, b):
    return lax.dot_general(a.astype(BF16), b.astype(BF16), (((0,), (0,)), ((), ())),
                           preferred_element_type=F32)


def _split2(a):
    hi = a.astype(BF16)
    lo = (a - hi.astype(F32)).astype(BF16)
    return hi, lo


def _dot3(a, b):
    ah, al = _split2(a)
    bh, bl = _split2(b)
    mm = functools.partial(jnp.dot, preferred_element_type=F32)
    return mm(ah, bh) + (mm(ah, bl) + mm(al, bh))


def _norm_mod(x, gain, scale, shift):
    ms = jnp.mean(x * x, axis=-1, keepdims=True)
    y = x * lax.rsqrt(ms + NORM_EPS) * gain
    return y * (1.0 + scale) + shift


def _mod_kernel(c_ref, w_ref, b_ref, o_ref):
    s = _silu(c_ref[...])
    o_ref[...] = jnp.dot(s, w_ref[...], precision=lax.Precision.HIGHEST,
                         preferred_element_type=F32) + b_ref[...]


def _adaln(c, w, b):
    bsz, d = c.shape
    n = w.shape[1]
    tn = 512
    return pl.pallas_call(
        _mod_kernel,
        grid=(n // tn,),
        in_specs=[pl.BlockSpec((bsz, d), lambda j: (0, 0)),
                  pl.BlockSpec((d, tn), lambda j: (0, j)),
                  pl.BlockSpec((1, tn), lambda j: (0, j))],
        out_specs=pl.BlockSpec((bsz, tn), lambda j: (0, j)),
        out_shape=jax.ShapeDtypeStruct((bsz, n), F32),
        compiler_params=_cparams(("arbitrary",)),
        name="adaln_mod",
    )(c, w, b.reshape(1, n))


A_IN_TM = 512
A_IN_TN = 512


def _a_in_kernel(x_ref, shift_ref, scale_ref, g_ref, w_ref, wab_ref, qkv_ref, z_ref, ab_ref):
    h = _norm_mod(x_ref[0], g_ref[...], scale_ref[0], shift_ref[0]).astype(BF16)
    for j in range(A_CONV_CH // A_IN_TN):
        sl = slice(j * A_IN_TN, (j + 1) * A_IN_TN)
        qkv_ref[0, :, sl] = jnp.dot(h, w_ref[:, sl], preferred_element_type=F32).astype(BF16)
    for j in range(A_V // A_IN_TN):
        sl = slice(j * A_IN_TN, (j + 1) * A_IN_TN)
        wsl = slice(A_CONV_CH + j * A_IN_TN, A_CONV_CH + (j + 1) * A_IN_TN)
        z_ref[0, :, sl] = jnp.dot(h, w_ref[:, wsl], preferred_element_type=F32).astype(BF16)
    ab_ref[0] = jnp.dot(h, wab_ref[...], preferred_element_type=F32)


def _a_in_proj(x, shift, scale, gain, w_main, w_ab):
    bsz, seq, d = x.shape
    tm = A_IN_TM
    row = lambda b, i: (b, i, 0)
    vec = lambda b, i: (b, 0, 0)
    const = lambda b, i: (0, 0)
    return pl.pallas_call(
        _a_in_kernel,
        grid=(bsz, seq // tm),
        in_specs=[pl.BlockSpec((1, tm, d), row),
                  pl.BlockSpec((1, 1, d), vec),
                  pl.BlockSpec((1, 1, d), vec),
                  pl.BlockSpec((1, d), const),
                  pl.BlockSpec(w_main.shape, const),
                  pl.BlockSpec(w_ab.shape, const)],
        out_specs=[pl.BlockSpec((1, tm, A_CONV_CH), row),
                   pl.BlockSpec((1, tm, A_V), row),
                   pl.BlockSpec((1, tm, LANES), row)],
        out_shape=[jax.ShapeDtypeStruct((bsz, seq, A_CONV_CH), BF16),
                   jax.ShapeDtypeStruct((bsz, seq, A_V), BF16),
                   jax.ShapeDtypeStruct((bsz, seq, LANES), F32)],
        compiler_params=_cparams(("parallel", "parallel")),
        name="a_in_proj",
    )(x, shift, scale, gain, w_main, w_ab)


def _inv_unit_lower(a_list, row, col):
    c = a_list[0].shape[0]
    eye = jnp.where(row == col, 1.0, 0.0).astype(F32)

    def blk(shift):
        return (row >> shift) == (col >> shift)

    b_prev = blk(3)
    ads = [jnp.where(b_prev, a, 0.0) for a in a_list]
    adb = [ad.astype(BF16) for ad in ads]
    a2s = [_dot(x, x) for x in adb]
    a2b = [a2.astype(BF16) for a2 in a2s]
    a3s = [_dot(x, y) for x, y in zip(adb, a2b)]
    a4s = [_dot(y, y) for y in a2b]
    p1s = [eye - ad + a2 - a3 for ad, a2, a3 in zip(ads, a2s, a3s)]
    ts = [p1 + _dot(p1, a4) for p1, a4 in zip(p1s, a4s)]
    shift = 4
    while (1 << shift) <= c:
        b_cur = blk(shift)
        sel = jnp.logical_and(b_cur, jnp.logical_not(b_prev))
        offs = [jnp.where(sel, a, 0.0).astype(BF16) for a in a_list]
        tbs = [t.astype(BF16) for t in ts]
        xs = [_dot(off, tb) for off, tb in zip(offs, tbs)]
        ts = [t - _dot(tb, x) for t, tb, x in zip(ts, tbs, xs)]
        b_prev = b_cur
        shift += 1
    rs = [eye - t - _dot3(a, t) for a, t in zip(a_list, ts)]
    return [t + _dot(t, r) for t, r in zip(ts, rs)]


def _gdn_kernel(qkv_ref, z_ref, ab_ref, convw_ref, alog_ref, dtb_ref, ogain_ref, o_ref,
                xe_ref, state_ref):
    c = GDN_CHUNK
    t_idx = pl.program_id(1)

    @pl.when(t_idx == 0)
    def _():
        xe_ref[0:SUBLANES, :] = jnp.zeros((SUBLANES, A_CONV_CH), F32)
        state_ref[...] = jnp.zeros_like(state_ref)

    x = qkv_ref[0].astype(F32)
    xe_ref[SUBLANES:SUBLANES + c, :] = x
    y = convw_ref[A_CONV - 1:A_CONV, :] * x
    for k in range(A_CONV - 1):
        off = SUBLANES - (A_CONV - 1) + k
        y = y + convw_ref[k:k + 1, :] * xe_ref[off:off + c, :]
    xe_ref[0:SUBLANES, :] = x[c - SUBLANES:c, :]
    y = _silu(y)

    ab = ab_ref[0]
    g = -jnp.exp(alog_ref[...]) * _softplus(ab + dtb_ref[...])
    beta = 1.0 / (1.0 + jnp.exp(-ab))
    row = lax.broadcasted_iota(jnp.int32, (c, c), 0)
    col = lax.broadcasted_iota(jnp.int32, (c, c), 1)
    tril = jnp.where(row >= col, 1.0, 0.0).astype(F32)
    gc = jnp.dot(tril, g, precision=lax.Precision.HIGHEST, preferred_element_type=F32)
    gc_t = gc.T
    eg = jnp.exp(gc)
    g_last = gc[c - 1:c, :]
    e_last = jnp.exp(g_last)
    e_rest = jnp.exp(g_last - gc)
    lower = row >= col
    strict = row > col

    heads = range(A_HEADS)
    qs, ks, kbs, rhss, decs = [], [], [], [], []
    for h in heads:
        q = y[:, h * A_DK:(h + 1) * A_DK]
        k = y[:, A_QK + h * A_DK:A_QK + (h + 1) * A_DK]
        v = y[:, 2 * A_QK + h * A_DV:2 * A_QK + (h + 1) * A_DV]
        q = q * (lax.rsqrt(jnp.sum(q * q, axis=-1, keepdims=True) + NORM_EPS) * (A_DK ** -0.5))
        k = k * lax.rsqrt(jnp.sum(k * k, axis=-1, keepdims=True) + NORM_EPS)
        b_col = beta[:, A_HEADS + h:A_HEADS + h + 1]
        kb = k * b_col
        diff = gc[:, h:h + 1] - gc_t[h:h + 1, :]
        decs.append(jnp.exp(jnp.where(lower, diff, -jnp.inf)))
        qs.append(q)
        ks.append(k)
        kbs.append(kb)
        rhss.append(jnp.concatenate([v * b_col, kb * eg[:, h:h + 1]], axis=1))
    ms = [_dot_nt(jnp.concatenate([kb, q], axis=0), k) for kb, q, k in zip(kbs, qs, ks)]
    a_list = [jnp.where(strict, m[:c] * dec, 0.0) for m, dec in zip(ms, decs)]
    qks = [m[c:] * dec for m, dec in zip(ms, decs)]
    t_invs = _inv_unit_lower(a_list, row, col)
    sols = [_dot(t, rhs) for t, rhs in zip(t_invs, rhss)]
    s_olds = [state_ref[h] for h in heads]
    wqs = [_dot(jnp.concatenate([sol[:, A_DV:], qs[h] * eg[:, h:h + 1]], axis=0), s_olds[h])
           for h, sol in zip(heads, sols)]
    v_news = [sol[:, :A_DV] - wq[:c] for sol, wq in zip(sols, wqs)]
    outs = [wq[c:] + _dot(qk, vn) for wq, qk, vn in zip(wqs, qks, v_news)]
    upds = [_dot_tn(ks[h] * e_rest[:, h:h + 1], v_news[h]) for h in heads]
    for h in heads:
        state_ref[h] = s_olds[h] * e_last[:, h:h + 1] + upds[h]
    for h in heads:
        sl = slice(h * A_DV, (h + 1) * A_DV)
        o = outs[h]
        o = o * lax.rsqrt(jnp.mean(o * o, axis=-1, keepdims=True) + NORM_EPS) * ogain_ref[...]
        zz = z_ref[0, :, sl].astype(F32)
        o_ref[0, :, sl] = (o * _silu(zz)).astype(BF16)


def _gdn_mixer(qkv, z, ab, conv_w, a_log_pad, dt_bias_pad, o_gain):
    bsz, seq, _ = qkv.shape
    c = GDN_CHUNK
    row = lambda b, i: (b, i, 0)
    const = lambda b, i: (0, 0)
    return pl.pallas_call(
        _gdn_kernel,
        grid=(bsz, seq // c),
        in_specs=[pl.BlockSpec((1, c, A_CONV_CH), row),
                  pl.BlockSpec((1, c, A_V), row),
                  pl.BlockSpec((1, c, LANES), row),
                  pl.BlockSpec((A_CONV, A_CONV_CH), const),
                  pl.BlockSpec((1, LANES), const),
                  pl.BlockSpec((1, LANES), const),
                  pl.BlockSpec((1, A_DV), const)],
        out_specs=pl.BlockSpec((1, c, A_V), row),
        out_shape=jax.ShapeDtypeStruct((bsz, seq, A_V), BF16),
        scratch_shapes=[pltpu.VMEM((SUBLANES + c, A_CONV_CH), F32),
                        pltpu.VMEM((A_HEADS, A_DK, A_DV), F32)],
        compiler_params=_cparams(("parallel", "arbitrary")),
        name="gdn_mixer",
    )(qkv, z, ab, conv_w, a_log_pad, dt_bias_pad, o_gain)


OUT_TM = 512


def _out_kernel(og_ref, x_ref, gate_ref, g_ref, w_ref, o_ref):
    y = jnp.dot(og_ref[0], w_ref[...], preferred_element_type=F32)
    yn = y * lax.rsqrt(jnp.mean(y * y, axis=-1, keepdims=True) + NORM_EPS) * g_ref[...]
    o_ref[0] = x_ref[0] + gate_ref[0] * yn


def _out_proj(og, x, gate, gain, w):
    bsz, seq, d = x.shape
    tm = OUT_TM
    row = lambda b, i: (b, i, 0)
    vec = lambda b, i: (b, 0, 0)
    const = lambda b, i: (0, 0)
    return pl.pallas_call(
        _out_kernel,
        grid=(bsz, seq // tm),
        in_specs=[pl.BlockSpec((1, tm, og.shape[-1]), row),
                  pl.BlockSpec((1, tm, d), row),
                  pl.BlockSpec((1, 1, d), vec),
                  pl.BlockSpec((1, d), const),
                  pl.BlockSpec(w.shape, const)],
        out_specs=pl.BlockSpec((1, tm, d), row),
        out_shape=jax.ShapeDtypeStruct((bsz, seq, d), F32),
        compiler_params=_cparams(("parallel", "parallel")),
        name="out_proj",
    )(og, x, gate, gain, w)


KV_TM = 512
KV_HEAD_GROUP = 4
ATT_T = 256


def _fparts(f):
    hi = f.astype(BF16).astype(F32)
    r1 = f - hi
    mid = r1.astype(BF16).astype(F32)
    lo = (r1 - mid).astype(BF16).astype(F32)
    lane = lax.broadcasted_iota(jnp.int32, f.shape, 1)
    out = jnp.where(lane < B_HEADS, hi, 0.0)
    out = out + jnp.where(jnp.logical_and(lane >= B_HEADS, lane < 2 * B_HEADS),
                          pltpu.roll(mid, B_HEADS, 1), 0.0)
    out = out + jnp.where(jnp.logical_and(lane >= 2 * B_HEADS, lane < 3 * B_HEADS),
                          pltpu.roll(lo, 2 * B_HEADS, 1), 0.0)
    return out


def _kv_kernel(x_ref, shift_ref, scale_ref, g_ref, wk_ref, wvt_ref, wf_ref, bf_ref, place_ref,
               ones_ref, k_ref, vt_ref, fcat_ref, carry_ref):
    tm = x_ref.shape[1]

    @pl.when(pl.program_id(1) == 0)
    def _():
        carry_ref[...] = jnp.zeros_like(carry_ref)

    h = _norm_mod(x_ref[0], g_ref[...], scale_ref[0], shift_ref[0]).astype(BF16)
    for grp in range(B_HEADS // KV_HEAD_GROUP):
        rows = slice(grp * KV_HEAD_GROUP * AUG_W, (grp + 1) * KV_HEAD_GROUP * AUG_W)
        vt = lax.dot_general(wvt_ref[rows, :], h, (((1,), (1,)), ((), ())),
                             preferred_element_type=F32)
        r = lax.broadcasted_iota(jnp.int32, vt.shape, 0)
        vt = jnp.where((r & (AUG_W - 1)) == B_DH, 1.0, vt).astype(BF16)
        for hh in range(KV_HEAD_GROUP):
            for jb in range(tm // ATT_T):
                vt_ref[0, grp * KV_HEAD_GROUP + hh, jb] = vt[hh * AUG_W:(hh + 1) * AUG_W,
                                                             jb * ATT_T:(jb + 1) * ATT_T]
    f =jnp.dot(h, wf_ref[...], preferred_element_type=F32) + bf_ref[...]
    lane = lax.broadcasted_iota(jnp.int32, f.shape, 1)
    log_f = jnp.where(lane < B_HEADS, -_softplus(-f), 0.0)
    row = lax.broadcasted_iota(jnp.int32, (tm, tm), 0)
    col = lax.broadcasted_iota(jnp.int32, (tm, tm), 1)
    tril = jnp.where(row >= col, 1.0, 0.0).astype(F32)
    f_cum = jnp.dot(tril, log_f, precision=lax.Precision.HIGHEST,
                    preferred_element_type=F32) + carry_ref[...]
    carry_ref[...] = f_cum[tm - 1:tm, :]
    fcat = _fparts(f_cum).astype(BF16)
    fcat_ref[0] = fcat
    for j in range(k_ref.shape[2] // 512):
        sl = slice(j * 512, (j + 1) * 512)
        k_aug = (jnp.dot(h, wk_ref[:, sl], preferred_element_type=F32)
                 + jnp.dot(fcat, place_ref[:, sl], preferred_element_type=F32) + ones_ref[:, sl])
        k_ref[0, :, sl] = k_aug.astype(BF16)


def _kv_proj(x, shift, scale, gain, wk_aug, wvt, wf, bf, place_k, ones_k):
    bsz, seq, d = x.shape
    tm = KV_TM
    row = lambda b, i: (b, i, 0)
    vec = lambda b, i: (b, 0, 0)
    const = lambda b, i: (0, 0)
    kw = B_HEADS * AUG_W
    nblk = tm // ATT_T
    return pl.pallas_call(
        _kv_kernel,
        grid=(bsz, seq // tm),
        in_specs=[pl.BlockSpec((1, tm, d), row),
                  pl.BlockSpec((1, 1, d), vec),
                  pl.BlockSpec((1, 1, d), vec),
                  pl.BlockSpec((1, d), const),
                  pl.BlockSpec(wk_aug.shape, const),
                  pl.BlockSpec(wvt.shape, const),
                  pl.BlockSpec(wf.shape, const),
                  pl.BlockSpec((1, LANES), const),
                  pl.BlockSpec(place_k.shape, const),
                  pl.BlockSpec((1, kw), const)],
        out_specs=[pl.BlockSpec((1, tm, kw), row),
                   pl.BlockSpec((1, B_HEADS, nblk, AUG_W, ATT_T), lambda b, i: (b, 0, i, 0, 0)),
                   pl.BlockSpec((1, tm, LANES), row)],
        out_shape=[jax.ShapeDtypeStruct((bsz, seq, kw), BF16),
                   jax.ShapeDtypeStruct((bsz, B_HEADS, seq // ATT_T, AUG_W, ATT_T), BF16),
                   jax.ShapeDtypeStruct((bsz, seq, LANES), BF16)],
        scratch_shapes=[pltpu.VMEM((1, LANES), F32)],
        compiler_params=_cparams(("parallel", "arbitrary")),
        name="kv_proj",
    )(x, shift, scale, gain, wk_aug, wvt, wf, bf, place_k, ones_k)


B_IN_TM = 512


def _b_in_kernel(x_ref, shift_ref, scale_ref, g_ref, wq_ref, wz_ref, fcat_ref, place_ref, ones_ref,
                 q_ref, z_ref):
    h = _norm_mod(x_ref[0], g_ref[...], scale_ref[0], shift_ref[0]).astype(BF16)
    fcat = fcat_ref[0]
    for j in range(q_ref.shape[2] // 512):
        sl = slice(j * 512, (j + 1) * 512)
        q_aug = (jnp.dot(h, wq_ref[:, sl], preferred_element_type=F32)
                 + jnp.dot(fcat, place_ref[:, sl], preferred_element_type=F32) + ones_ref[:, sl])
        q_ref[0, :, sl] = q_aug.astype(BF16)
    for j in range(z_ref.shape[2] // 512):
        sl = slice(j * 512, (j + 1) * 512)
        z_ref[0, :, sl] = jnp.dot(h, wz_ref[:, sl], preferred_element_type=F32).astype(BF16)


def _b_in_proj(x, shift, scale, gain, wq_aug, wz, fcat, place_q, ones_q):
    bsz, seq, d = x.shape
    tm = B_IN_TM
    row = lambda b, i: (b, i, 0)
    vec = lambda b, i: (b, 0, 0)
    const = lambda b, i: (0, 0)
    qw = B_HEADS * AUG_W
    return pl.pallas_call(
        _b_in_kernel,
        grid=(bsz, seq // tm),
        in_specs=[pl.BlockSpec((1, tm, d), row),
                  pl.BlockSpec((1, 1, d), vec),
                  pl.BlockSpec((1, 1, d), vec),
                  pl.BlockSpec((1, d), const),
                  pl.BlockSpec(wq_aug.shape, const),
                  pl.BlockSpec(wz.shape, const),
                  pl.BlockSpec((1, tm, LANES), row),
                  pl.BlockSpec(place_q.shape, const),
                  pl.BlockSpec((1, qw), const)],
        out_specs=[pl.BlockSpec((1, tm, qw), row),
                   pl.BlockSpec((1, tm, B_W), row)],
        out_shape=[jax.ShapeDtypeStruct((bsz, seq, qw), BF16),
                   jax.ShapeDtypeStruct((bsz, seq, B_W), BF16)],
        compiler_params=_cparams(("parallel", "parallel")),
        name="b_in_proj",
    )(x, shift, scale, gain, wq_aug, wz, fcat, place_q, ones_q)


HEADS_PER_STEP = 2
ATT_KB = 2


def _attn_kernel(q_ref, k_ref, vt_ref, z_ref, o_ref):
    t = ATT_T
    i = pl.program_id(2)
    qs = [q_ref[0, :, hh * AUG_W:(hh + 1) * AUG_W] for hh in range(HEADS_PER_STEP)]

    hds = range(HEADS_PER_STEP)

    def step(jb, nb, carry, masked):
        start = pl.multiple_of(jb * t, t)
        ss = [lax.dot_general(k_ref[0, pl.ds(start, nb * t), hh * AUG_W:(hh + 1) * AUG_W], qs[hh],
                              (((1,), (1,)), ((), ())), preferred_element_type=F32) for hh in hds]
        if masked:
            key = lax.broadcasted_iota(jnp.int32, (nb * t, t), 0)
            qry = lax.broadcasted_iota(jnp.int32, (nb * t, t), 1)
            ss = [jnp.where(key <= qry, s, -jnp.inf) for s in ss]
        m_news = [jnp.maximum(carry[hh][0], jnp.max(ss[hh], axis=0, keepdims=True)) for hh in hds]
        ps = [jnp.exp(ss[hh] - m_news[hh]).astype(BF16) for hh in hds]
        alphas = [jnp.exp(carry[hh][0] - m_news[hh]) for hh in hds]
        pvs = []
        for hh in hds:
            pv = jnp.dot(vt_ref[0, hh, jb], ps[hh][0:t], preferred_element_type=F32)
            for b in range(1, nb):
                pv = pv + jnp.dot(vt_ref[0, hh, jb + b], ps[hh][b * t:(b + 1) * t],
                                  preferred_element_type=F32)
            pvs.append(pv)
        return tuple((m_news[hh], alphas[hh] * carry[hh][1] + pvs[hh]) for hh in hds)

    init = tuple((jnp.full((1, t), -jnp.inf, F32), jnp.zeros((AUG_W, t), F32)) for _ in hds)
    carry = lax.fori_loop(0, i // ATT_KB, lambda j, cr: step(j * ATT_KB, ATT_KB, cr, False), init)
    for r in range(ATT_KB - 1, 0, -1):
        carry = lax.cond((i % ATT_KB) >= r, lambda cr, r=r: step(i - r, 1, cr, False),
                         lambda cr: cr, carry)
    carry = step(i, 1, carry, True)
    o_t = jnp.concatenate([acc[:B_DH, :] / acc[B_DH:B_DH + 1, :] for (_, acc) in carry], axis=0)
    o_ref[0] = (o_t.T * _silu(z_ref[0].astype(F32))).astype(BF16)


def _fox_attention(q_aug, k_aug, vt, z):
    bsz, seq, _ = z.shape
    t = ATT_T
    n_pairs = B_HEADS // HEADS_PER_STEP
    pw = HEADS_PER_STEP * AUG_W
    return pl.pallas_call(
        _attn_kernel,
        grid=(bsz, n_pairs, seq // t),
        in_specs=[pl.BlockSpec((1, t, pw), lambda b, p, i: (b, i, p)),
                  pl.BlockSpec((1, seq, pw), lambda b, p, i: (b, 0, p)),
                  pl.BlockSpec((1, HEADS_PER_STEP, seq // t, AUG_W, t),
                               lambda b, p, i: (b, p, 0, 0, 0)),
                  pl.BlockSpec((1, t, LANES), lambda b, p, i: (b, i, p))],
        out_specs=pl.BlockSpec((1, t, LANES), lambda b, p, i: (b, i, p)),
        out_shape=jax.ShapeDtypeStruct((bsz, seq, B_W), BF16),
        compiler_params=_cparams(("parallel", "parallel", "arbitrary")),
        name="fox_attention",
    )(q_aug, k_aug, vt, z)


def _pad_cols(w, n):
    return jnp.pad(w, ((0, 0), (0, n - w.shape[1])))


def _aug_weight(w, scale):
    d = w.shape[0]
    w = (w * scale).reshape(d, B_HEADS, B_DH)
    w = jnp.pad(w, ((0, 0), (0, 0), (0, AUG_W - B_DH)))
    return w.reshape(d, B_HEADS * AUG_W).astype(BF16)


def _placement(f_off, one_off, sign):
    place = np.zeros((LANES, B_HEADS * AUG_W), np.float32)
    ones = np.zeros((1, B_HEADS * AUG_W), np.float32)
    for p in range(N_FPARTS):
        for h in range(B_HEADS):
            place[p * B_HEADS + h, h * AUG_W + f_off + p] = sign
            ones[0, h * AUG_W + one_off + p] = 1.0
    return jnp.asarray(place, BF16), jnp.asarray(ones, F32)


def _vec3(m, n):
    return [t[:, None, :] for t in jnp.split(m, n, axis=-1)]


def kernel(x, c, a_mod_w, a_mod_b, a_pre_g, a_post_g, a_w_in, a_conv_w, a_a_log, a_dt_bias, a_o_gain, a_w_out, kv_mod_w, kv_mod_b, kv_norm_g, kv_w, kv_b_f, b_mod_w, b_mod_b, b_pre_g, b_post_g, b_w_in, b_w_out):
    place_q, ones_q = _placement(AUG_F0, AUG_F0 + N_FPARTS, 1.0)
    place_k, ones_k = _placement(AUG_F0 + N_FPARTS, AUG_F0, -1.0)

    for j in range(N_A_LAYERS):
        shift, scale, gate = _vec3(_adaln(c, a_mod_w[j], a_mod_b[j]), 3)
        w_in = a_w_in[j]
        n_main = A_CONV_CH + A_V
        w_main = w_in[:, :n_main].astype(BF16)
        w_ab = _pad_cols(w_in[:, n_main:], LANES).astype(BF16)
        qkv, z, ab = _a_in_proj(x, shift, scale, a_pre_g[j][None, :], w_main, w_ab)
        a_log_pad = _pad_cols(a_a_log[j][None, :], LANES)
        dt_pad = _pad_cols(a_dt_bias[j][None, :], LANES)
        og = _gdn_mixer(qkv, z, ab, a_conv_w[j], a_log_pad, dt_pad, a_o_gain[j][None, :])
        x = _out_proj(og, x, gate, a_post_g[j][None, :], a_w_out[j].astype(BF16))

    shift, scale = _vec3(_adaln(c, kv_mod_w, kv_mod_b), 2)
    wk_aug = _aug_weight(kv_w[:, :B_W], 1.0)
    wvt = _aug_weight(kv_w[:, B_W:2 * B_W], 1.0).T
    wf = _pad_cols(kv_w[:, 2 * B_W:], LANES).astype(BF16)
    bf = _pad_cols(kv_b_f[None, :], LANES)
    k_aug, vt, fcat = _kv_proj(x, shift, scale, kv_norm_g[None, :], wk_aug, wvt, wf, bf, place_k, ones_k)

    for j in range(N_B_LAYERS):
        shift, scale, gate = _vec3(_adaln(c, b_mod_w[j], b_mod_b[j]), 3)
        wq_aug = _aug_weight(b_w_in[j][:, :B_W], B_DH ** -0.5)
        wz = b_w_in[j][:, B_W:].astype(BF16)
        q_aug, z = _b_in_proj(x, shift, scale, b_pre_g[j][None, :], wq_aug, wz, fcat, place_q, ones_q)
        og = _fox_attention(q_aug, k_aug, vt, z)
        x = _out_proj(og, x, gate, b_post_g[j][None, :], b_w_out[j].astype(BF16))
    return x
```

```python
import functools

import jax
import jax.numpy as jnp
import numpy as np
from jax import lax
from jax.experimental import pallas as pl
from jax.experimental.pallas import tpu as pltpu

F32 = jnp.float32
BF16 = jnp.bfloat16

D_MODEL = 1024
N_A_LAYERS = 2
N_B_LAYERS = 2
A_HEADS = 8
A_DK = 128
A_DV = 128
A_QK = A_HEADS * A_DK
A_V = A_HEADS * A_DV
A_CONV = 4
A_CONV_CH = 2 * A_QK + A_V
B_HEADS = 16
B_DH = 64
B_W = B_HEADS * B_DH
NORM_EPS = 1e-6

LANES = 128
SUBLANES = 8
GDN_CHUNK = 128
VMEM_LIMIT = 56 * 1024 * 1024

AUG_W = 128
AUG_F0 = B_DH
N_FPARTS = 3


def _cparams(sem):
    return pltpu.CompilerParams(dimension_semantics=sem, vmem_limit_bytes=VMEM_LIMIT)


def _silu(x):
    return x / (1.0 + jnp.exp(-x))


def _softplus(x):
    return jnp.maximum(x, 0.0) + jnp.log(1.0 + jnp.exp(-jnp.abs(x)))


def _dot(a, b):
    return jnp.dot(a.astype(BF16), b.astype(BF16), preferred_element_type=F32)


def _dot_nt(a, b):
    return lax.dot_general(a.astype(BF16), b.astype(BF16), (((1,), (1,)), ((), ())),
                           preferred_element_type=F32)


def _dot_tn(a, b):
    return lax.dot_general(a.astype(BF16), b.astype(BF16), (((0,), (0,)), ((), ())),
                           preferred_element_type=F32)


def _split2(a):
    hi = a.astype(BF16)
    lo = (a - hi.astype(F32)).astype(BF16)
    return hi, lo


def _dot3(a, b):
    ah, al = _split2(a)
    bh, bl = _split2(b)
    mm = functools.partial(jnp.dot, preferred_element_type=F32)
    return mm(ah, bh) + (mm(ah, bl) + mm(al, bh))


def _norm_mod(x, gain, scale, shift):
    ms = jnp.mean(x * x, axis=-1, keepdims=True)
    y = x * lax.rsqrt(ms + NORM_EPS) * gain
    return y * (1.0 + scale) + shift


def _mod_kernel(c_ref, w_ref, b_ref, o_ref):
    s = _silu(c_ref[...])
    o_ref[...] = jnp.dot(s, w_ref[...], precision=lax.Precision.HIGHEST,
                         preferred_element_type=F32) + b_ref[...]


def _adaln(c, w, b):
    bsz, d = c.shape
    n = w.shape[1]
    tn = 512
    return pl.pallas_call(
        _mod_kernel,
        grid=(n // tn,),
        in_specs=[pl.BlockSpec((bsz, d), lambda j: (0, 0)),
                  pl.BlockSpec((d, tn), lambda j: (0, j)),
                  pl.BlockSpec((1, tn), lambda j: (0, j))],
        out_specs=pl.BlockSpec((bsz, tn), lambda j: (0, j)),
        out_shape=jax.ShapeDtypeStruct((bsz, n), F32),
        compiler_params=_cparams(("arbitrary",)),
        name="adaln_mod",
    )(c, w, b.reshape(1, n))


A_IN_TM = 512
A_IN_TN = 512


def _a_in_kernel(x_ref, shift_ref, scale_ref, g_ref, w_ref, wab_ref, qkv_ref, z_ref, ab_ref):
    h = _norm_mod(x_ref[0], g_ref[...], scale_ref[0], shift_ref[0]).astype(BF16)
    for j in range(A_CONV_CH // A_IN_TN):
        sl = slice(j * A_IN_TN, (j + 1) * A_IN_TN)
        qkv_ref[0, :, sl] = jnp.dot(h, w_ref[:, sl], preferred_element_type=F32).astype(BF16)
    for j in range(A_V // A_IN_TN):
        sl = slice(j * A_IN_TN, (j + 1) * A_IN_TN)
        wsl = slice(A_CONV_CH + j * A_IN_TN, A_CONV_CH + (j + 1) * A_IN_TN)
        z_ref[0, :, sl] = jnp.dot(h, w_ref[:, wsl], preferred_element_type=F32).astype(BF16)
    ab_ref[0] = jnp.dot(h, wab_ref[...], preferred_element_type=F32)


def _a_in_proj(x, shift, scale, gain, w_main, w_ab):
    bsz, seq, d = x.shape
    tm = A_IN_TM
    row = lambda b, i: (b, i, 0)
    vec = lambda b, i: (b, 0, 0)
    const = lambda b, i: (0, 0)
    return pl.pallas_call(
        _a_in_kernel,
        grid=(bsz, seq // tm),
        in_specs=[pl.BlockSpec((1, tm, d), row),
                  pl.BlockSpec((1, 1, d), vec),
                  pl.BlockSpec((1, 1, d), vec),
                  pl.BlockSpec((1, d), const),
                  pl.BlockSpec(w_main.shape, const),
                  pl.BlockSpec(w_ab.shape, const)],
        out_specs=[pl.BlockSpec((1, tm, A_CONV_CH), row),
                   pl.BlockSpec((1, tm, A_V), row),
                   pl.BlockSpec((1, tm, LANES), row)],
        out_shape=[jax.ShapeDtypeStruct((bsz, seq, A_CONV_CH), BF16),
                   jax.ShapeDtypeStruct((bsz, seq, A_V), BF16),
                   jax.ShapeDtypeStruct((bsz, seq, LANES), F32)],
        compiler_params=_cparams(("parallel", "parallel")),
        name="a_in_proj",
    )(x, shift, scale, gain, w_main, w_ab)


def _inv_unit_lower(a_list, row, col):
    c = a_list[0].shape[0]
    eye = jnp.where(row == col, 1.0, 0.0).astype(F32)

    def blk(shift):
        return (row >> shift) == (col >> shift)

    b_prev = blk(3)
    ads = [jnp.where(b_prev, a, 0.0) for a in a_list]
    adb = [ad.astype(BF16) for ad in ads]
    a2s = [_dot(x, x) for x in adb]
    a2b = [a2.astype(BF16) for a2 in a2s]
    a3s = [_dot(x, y) for x, y in zip(adb, a2b)]
    a4s = [_dot(y, y) for y in a2b]
    p1s = [eye - ad + a2 - a3 for ad, a2, a3 in zip(ads, a2s, a3s)]
    ts = [p1 + _dot(p1, a4) for p1, a4 in zip(p1s, a4s)]
    shift = 4
    while (1 << shift) <= c:
        b_cur = blk(shift)
        sel = jnp.logical_and(b_cur, jnp.logical_not(b_prev))
        offs = [jnp.where(sel, a, 0.0).astype(BF16) for a in a_list]
        tbs = [t.astype(BF16) for t in ts]
        xs = [_dot(off, tb) for off, tb in zip(offs, tbs)]
        ts = [t - _dot(tb, x) for t, tb, x in zip(ts, tbs, xs)]
        b_prev = b_cur
        shift += 1
    rs = [eye - t - _dot3(a, t) for a, t in zip(a_list, ts)]
    return [t + _dot(t, r) for t, r in zip(ts, rs)]


def _gdn_kernel(qkv_ref, z_ref, ab_ref, convw_ref, alog_ref, dtb_ref, ogain_ref, o_ref,
                xe_ref, state_ref):
    c = GDN_CHUNK
    t_idx = pl.program_id(1)

    @pl.when(t_idx == 0)
    def _():
        xe_ref[0:SUBLANES, :] = jnp.zeros((SUBLANES, A_CONV_CH), F32)
        state_ref[...] = jnp.zeros_like(state_ref)

    x = qkv_ref[0].astype(F32)
    xe_ref[SUBLANES:SUBLANES + c, :] = x
    y = convw_ref[A_CONV - 1:A_CONV, :] * x
    for k in range(A_CONV - 1):
        off = SUBLANES - (A_CONV - 1) + k
        y = y + convw_ref[k:k + 1, :] * xe_ref[off:off + c, :]
    xe_ref[0:SUBLANES, :] = x[c - SUBLANES:c, :]
    y = _silu(y)

    ab = ab_ref[0]
    g = -jnp.exp(alog_ref[...]) * _softplus(ab + dtb_ref[...])
    beta = 1.0 / (1.0 + jnp.exp(-ab))
    row = lax.broadcasted_iota(jnp.int32, (c, c), 0)
    col = lax.broadcasted_iota(jnp.int32, (c, c), 1)
    tril = jnp.where(row >= col, 1.0, 0.0).astype(F32)
    gc = jnp.dot(tril, g, precision=lax.Precision.HIGHEST, preferred_element_type=F32)
    gc_t = gc.T
    eg = jnp.exp(gc)
    g_last = gc[c - 1:c, :]
    e_last = jnp.exp(g_last)
    e_rest = jnp.exp(g_last - gc)
    lower = row >= col
    strict = row > col

    heads = range(A_HEADS)
    qs, ks, kbs, rhss, decs = [], [], [], [], []
    for h in heads:
        q = y[:, h * A_DK:(h + 1) * A_DK]
        k = y[:, A_QK + h * A_DK:A_QK + (h + 1) * A_DK]
        v = y[:, 2 * A_QK + h * A_DV:2 * A_QK + (h + 1) * A_DV]
        q = q * (lax.rsqrt(jnp.sum(q * q, axis=-1, keepdims=True) + NORM_EPS) * (A_DK ** -0.5))
        k = k * lax.rsqrt(jnp.sum(k * k, axis=-1, keepdims=True) + NORM_EPS)
        b_col = beta[:, A_HEADS + h:A_HEADS + h + 1]
        kb = k * b_col
        diff = gc[:, h:h + 1] - gc_t[h:h + 1, :]
        decs.append(jnp.exp(jnp.where(lower, diff, -jnp.inf)))
        qs.append(q)
        ks.append(k)
        kbs.append(kb)
        rhss.append(jnp.concatenate([v * b_col, kb * eg[:, h:h + 1]], axis=1))
    ms = [_dot_nt(jnp.concatenate([kb, q], axis=0), k) for kb, q, k in zip(kbs, qs, ks)]
    a_list = [jnp.where(strict, m[:c] * dec, 0.0) for m, dec in zip(ms, decs)]
    qks = [m[c:] * dec for m, dec in zip(ms, decs)]
    t_invs = _inv_unit_lower(a_list, row, col)
    sols = [_dot(t, rhs) for t, rhs in zip(t_invs, rhss)]
    s_olds = [state_ref[h] for h in heads]
    wqs = [_dot(jnp.concatenate([sol[:, A_DV:], qs[h] * eg[:, h:h + 1]], axis=0), s_olds[h])
           for h, sol in zip(heads, sols)]
    v_news = [sol[:, :A_DV] - wq[:c] for sol, wq in zip(sols, wqs)]
    outs = [wq[c:] + _dot(qk, vn) for wq, qk, vn in zip(wqs, qks, v_news)]
    upds = [_dot_tn(ks[h] * e_rest[:, h:h + 1], v_news[h]) for h in heads]
    for h in heads:
        state_ref[h] = s_olds[h] * e_last[:, h:h + 1] + upds[h]
    for h in heads:
        sl = slice(h * A_DV, (h + 1) * A_DV)
        o = outs[h]
        o = o * lax.rsqrt(jnp.mean(o * o, axis=-1, keepdims=True) + NORM_EPS) * ogain_ref[...]
        zz = z_ref[0, :, sl].astype(F32)
        o_ref[0, :, sl] = (o * _silu(zz)).astype(BF16)


def _gdn_mixer(qkv, z, ab, conv_w, a_log_pad, dt_bias_pad, o_gain):
    bsz, seq, _ = qkv.shape
    c = GDN_CHUNK
    row = lambda b, i: (b, i, 0)
    const = lambda b, i: (0, 0)
    return pl.pallas_call(
        _gdn_kernel,
        grid=(bsz, seq // c),
        in_specs=[pl.BlockSpec((1, c, A_CONV_CH), row),
                  pl.BlockSpec((1, c, A_V), row),
                  pl.BlockSpec((1, c, LANES), row),
                  pl.BlockSpec((A_CONV, A_CONV_CH), const),
                  pl.BlockSpec((1, LANES), const),
                  pl.BlockSpec((1, LANES), const),
                  pl.BlockSpec((1, A_DV), const)],
        out_specs=pl.BlockSpec((1, c, A_V), row),
        out_shape=jax.ShapeDtypeStruct((bsz, seq, A_V), BF16),
        scratch_shapes=[pltpu.VMEM((SUBLANES + c, A_CONV_CH), F32),
                        pltpu.VMEM((A_HEADS, A_DK, A_DV), F32)],
        compiler_params=_cparams(("parallel", "arbitrary")),
        name="gdn_mixer",
    )(qkv, z, ab, conv_w, a_log_pad, dt_bias_pad, o_gain)


OUT_TM = 512


def _out_kernel(og_ref, x_ref, gate_ref, g_ref, w_ref, o_ref):
    y = jnp.dot(og_ref[0], w_ref[...], preferred_element_type=F32)
    yn = y * lax.rsqrt(jnp.mean(y * y, axis=-1, keepdims=True) + NORM_EPS) * g_ref[...]
    o_ref[0] = x_ref[0] + gate_ref[0] * yn


def _out_proj(og, x, gate, gain, w):
    bsz, seq, d = x.shape
    tm = OUT_TM
    row = lambda b, i: (b, i, 0)
    vec = lambda b, i: (b, 0, 0)
    const = lambda b, i: (0, 0)
    return pl.pallas_call(
        _out_kernel,
        grid=(bsz, seq // tm),
        in_specs=[pl.BlockSpec((1, tm, og.shape[-1]), row),
                  pl.BlockSpec((1, tm, d), row),
                  pl.BlockSpec((1, 1, d), vec),
                  pl.BlockSpec((1, d), const),
                  pl.BlockSpec(w.shape, const)],
        out_specs=pl.BlockSpec((1, tm, d), row),
        out_shape=jax.ShapeDtypeStruct((bsz, seq, d), F32),
        compiler_params=_cparams(("parallel", "parallel")),
        name="out_proj",
    )(og, x, gate, gain, w)


KV_TM = 512
KV_HEAD_GROUP = 4
ATT_T = 512


def _fparts(f):
    hi = f.astype(BF16).astype(F32)
    r1 = f - hi
    mid = r1.astype(BF16).astype(F32)
    lo = (r1 - mid).astype(BF16).astype(F32)
    lane = lax.broadcasted_iota(jnp.int32, f.shape, 1)
    out = jnp.where(lane < B_HEADS, hi, 0.0)
    out = out + jnp.where(jnp.logical_and(lane >= B_HEADS, lane < 2 * B_HEADS),
                          pltpu.roll(mid, B_HEADS, 1), 0.0)
    out = out + jnp.where(jnp.logical_and(lane >= 2 * B_HEADS, lane < 3 * B_HEADS),
                          pltpu.roll(lo, 2 * B_HEADS, 1), 0.0)
    return out


def _kv_kernel(x_ref, shift_ref, scale_ref, g_ref, wk_ref, wvt_ref, wf_ref, bf_ref, place_ref,
               ones_ref, k_ref, vt_ref, fcat_ref, carry_ref):
    tm = x_ref.shape[1]

    @pl.when(pl.program_id(1) == 0)
    def _():
        carry_ref[...] = jnp.zeros_like(carry_ref)

    h = _norm_mod(x_ref[0], g_ref[...], scale_ref[0], shift_ref[0]).astype(BF16)
    for grp in range(B_HEADS // KV_HEAD_GROUP):
        rows = slice(grp * KV_HEAD_GROUP * AUG_W, (grp + 1) * KV_HEAD_GROUP * AUG_W)
        vt = lax.dot_general(wvt_ref[rows, :], h, (((1,), (1,)), ((), ())),
                             preferred_element_type=F32)
        r = lax.broadcasted_iota(jnp.int32, vt.shape, 0)
        vt = jnp.where((r & (AUG_W - 1)) == B_DH, 1.0, vt).astype(BF16)
        for hh in range(KV_HEAD_GROUP):
            for jb in range(tm // ATT_T):
                vt_ref[0, grp * KV_HEAD_GROUP + hh, jb] = vt[hh * AUG_W:(hh + 1) * AUG_W,
                                                             jb * ATT_T:(jb + 1) * ATT_T]
    f =jnp.dot(h, wf_ref[...], preferred_element_type=F32) + bf_ref[...]
    lane = lax.broadcasted_iota(jnp.int32, f.shape, 1)
    log_f = jnp.where(lane < B_HEADS, -_softplus(-f), 0.0)
    row = lax.broadcasted_iota(jnp.int32, (tm, tm), 0)
    col = lax.broadcasted_iota(jnp.int32, (tm, tm), 1)
    tril = jnp.where(row >= col, 1.0, 0.0).astype(F32)
    f_cum = jnp.dot(tril, log_f, precision=lax.Precision.HIGHEST,
                    preferred_element_type=F32) + carry_ref[...]
    carry_ref[...] = f_cum[tm - 1:tm, :]
    fcat = _fparts(f_cum).astype(BF16)
    fcat_ref[0] = fcat
    for j in range(k_ref.shape[2] // 512):
        sl = slice(j * 512, (j + 1) * 512)
        k_aug = (jnp.dot(h, wk_ref[:, sl], preferred_element_type=F32)
                 + jnp.dot(fcat, place_ref[:, sl], preferred_element_type=F32) + ones_ref[:, sl])
        k_ref[0, :, sl] = k_aug.astype(BF16)


def _kv_proj(x, shift, scale, gain, wk_aug, wvt, wf, bf, place_k, ones_k):
    bsz, seq, d = x.shape
    tm = KV_TM
    row = lambda b, i: (b, i, 0)
    vec = lambda b, i: (b, 0, 0)
    const = lambda b, i: (0, 0)
    kw = B_HEADS * AUG_W
    nblk = tm // ATT_T
    return pl.pallas_call(
        _kv_kernel,
        grid=(bsz, seq // tm),
        in_specs=[pl.BlockSpec((1, tm, d), row),
                  pl.BlockSpec((1, 1, d), vec),
                  pl.BlockSpec((1, 1, d), vec),
                  pl.BlockSpec((1, d), const),
                  pl.BlockSpec(wk_aug.shape, const),
                  pl.BlockSpec(wvt.shape, const),
                  pl.BlockSpec(wf.shape, const),
                  pl.BlockSpec((1, LANES), const),
                  pl.BlockSpec(place_k.shape, const),
                  pl.BlockSpec((1, kw), const)],
        out_specs=[pl.BlockSpec((1, tm, kw), row),
                   pl.BlockSpec((1, B_HEADS, nblk, AUG_W, ATT_T), lambda b, i: (b, 0, i, 0, 0)),
                   pl.BlockSpec((1, tm, LANES), row)],
        out_shape=[jax.ShapeDtypeStruct((bsz, seq, kw), BF16),
                   jax.ShapeDtypeStruct((bsz, B_HEADS, seq // ATT_T, AUG_W, ATT_T), BF16),
                   jax.ShapeDtypeStruct((bsz, seq, LANES), BF16)],
        scratch_shapes=[pltpu.VMEM((1, LANES), F32)],
        compiler_params=_cparams(("parallel", "arbitrary")),
        name="kv_proj",
    )(x, shift, scale, gain, wk_aug, wvt, wf, bf, place_k, ones_k)


B_IN_TM = 512


def _b_in_kernel(x_ref, shift_ref, scale_ref, g_ref, wq_ref, wz_ref, fcat_ref, place_ref, ones_ref,
                 q_ref, z_ref):
    h = _norm_mod(x_ref[0], g_ref[...], scale_ref[0], shift_ref[0]).astype(BF16)
    fcat = fcat_ref[0]
    for j in range(q_ref.shape[2] // 512):
        sl = slice(j * 512, (j + 1) * 512)
        q_aug = (jnp.dot(h, wq_ref[:, sl], preferred_element_type=F32)
                 + jnp.dot(fcat, place_ref[:, sl], preferred_element_type=F32) + ones_ref[:, sl])
        q_ref[0, :, sl] = q_aug.astype(BF16)
    for j in range(z_ref.shape[2] // 512):
        sl = slice(j * 512, (j + 1) * 512)
        z_ref[0, :, sl] = jnp.dot(h, wz_ref[:, sl], preferred_element_type=F32).astype(BF16)


def _b_in_proj(x, shift, scale, gain, wq_aug, wz, fcat, place_q, ones_q):
    bsz, seq, d = x.shape
    tm = B_IN_TM
    row = lambda b, i: (b, i, 0)
    vec = lambda b, i: (b, 0, 0)
    const = lambda b, i: (0, 0)
    qw = B_HEADS * AUG_W
    return pl.pallas_call(
        _b_in_kernel,
        grid=(bsz, seq // tm),
        in_specs=[pl.BlockSpec((1, tm, d), row),
                  pl.BlockSpec((1, 1, d), vec),
                  pl.BlockSpec((1, 1, d), vec),
                  pl.BlockSpec((1, d), const),
                  pl.BlockSpec(wq_aug.shape, const),
                  pl.BlockSpec(wz.shape, const),
                  pl.BlockSpec((1, tm, LANES), row),
                  pl.BlockSpec(place_q.shape, const),
                  pl.BlockSpec((1, qw), const)],
        out_specs=[pl.BlockSpec((1, tm, qw), row),
                   pl.BlockSpec((1, tm, B_W), row)],
        out_shape=[jax.ShapeDtypeStruct((bsz, seq, qw), BF16),
                   jax.ShapeDtypeStruct((bsz, seq, B_W), BF16)],
        compiler_params=_cparams(("parallel", "parallel")),
        name="b_in_proj",
    )(x, shift, scale, gain, wq_aug, wz, fcat, place_q, ones_q)


HEADS_PER_STEP = 4


def _attn_kernel(q_ref, k_ref, vt_ref, z_ref, o_ref, s_ref, p_ref, acc_ref):
    t = ATT_T
    i = pl.program_id(2)
    qs = [q_ref[0, :, hh * AUG_W:(hh + 1) * AUG_W] for hh in range(HEADS_PER_STEP)]

    hds = range(HEADS_PER_STEP)

    def scores(jb):
        start = pl.multiple_of(jb * t, t)
        return [lax.dot_general(k_ref[0, pl.ds(start, t), hh * AUG_W:(hh + 1) * AUG_W], qs[hh],
                                (((1,), (1,)), ((), ())), preferred_element_type=F32) for hh in hds]

    def weighted_values(jb, slot):
        return [jnp.dot(vt_ref[0, hh, jb], p_ref[slot, hh], preferred_element_type=F32)
                for hh in hds]

    def softmax_update(ms, masked, slot):
        ss = [s_ref[hh] for hh in hds]
        if masked:
            key = lax.broadcasted_iota(jnp.int32, (t, t), 0)
            qry = lax.broadcasted_iota(jnp.int32, (t, t), 1)
            ss = [jnp.where(key <= qry, s, -jnp.inf) for s in ss]
        m_news = [jnp.maximum(ms[hh], jnp.max(ss[hh], axis=0, keepdims=True)) for hh in hds]
        for hh in hds:
            p_ref[slot, hh] = jnp.exp(ss[hh] - m_news[hh]).astype(BF16)
        return m_news, [jnp.exp(ms[hh] - m_news[hh]) for hh in hds]

    def accumulate(alphas, pvs):
        for hh in hds:
            acc_ref[hh] = alphas[hh] * acc_ref[hh] + pvs[hh]

    def body(j, carry):
        ms, alphas = carry
        slot = j & 1
        pvs = weighted_values(jnp.maximum(j - 1, 0), 1 - slot)
        ss_next = scores(j + 1)
        ms, alphas_new = softmax_update(ms, False, slot)
        accumulate(alphas, pvs)
        for hh in hds:
            s_ref[hh] = ss_next[hh]
        return ms, alphas_new

    ss0 = scores(0)
    for hh in hds:
        s_ref[hh] = ss0[hh]
        p_ref[1, hh] = jnp.zeros((t, t), BF16)
        acc_ref[hh] = jnp.zeros((AUG_W, t), F32)
    init = ([jnp.full((1, t), -jnp.inf, F32) for _ in hds], [jnp.ones((1, t), F32) for _ in hds])
    ms, alphas = lax.fori_loop(0, i, body, init)
    slot = i & 1
    pvs = weighted_values(jnp.maximum(i - 1, 0), 1 - slot)
    ms, alphas_new = softmax_update(ms, True, slot)
    accumulate(alphas, pvs)
    accumulate(alphas_new, weighted_values(i, slot))
    o_t = jnp.concatenate([acc_ref[hh][:B_DH, :] / acc_ref[hh][B_DH:B_DH + 1, :] for hh in hds],
                          axis=0)
    o_ref[0] = (o_t.T * _silu(z_ref[0].astype(F32))).astype(BF16)


def _fox_attention(q_aug, k_aug, vt, z):
    bsz, seq, _ = z.shape
    t = ATT_T
    n_pairs = B_HEADS // HEADS_PER_STEP
    pw = HEADS_PER_STEP * AUG_W
    return pl.pallas_call(
        _attn_kernel,
        grid=(bsz, n_pairs, seq // t),
        in_specs=[pl.BlockSpec((1, t, pw), lambda b, p, i: (b, i, p)),
                  pl.BlockSpec((1, seq, pw), lambda b, p, i: (b, 0, p)),
                  pl.BlockSpec((1, HEADS_PER_STEP, seq // t, AUG_W, t),
                               lambda b, p, i: (b, p, 0, 0, 0)),
                  pl.BlockSpec((1, t, HEADS_PER_STEP * B_DH), lambda b, p, i: (b, i, p))],
        out_specs=pl.BlockSpec((1, t, HEADS_PER_STEP * B_DH), lambda b, p, i: (b, i, p)),
        out_shape=jax.ShapeDtypeStruct((bsz, seq, B_W), BF16),
        scratch_shapes=[pltpu.VMEM((HEADS_PER_STEP, t, t), F32),
                        pltpu.VMEM((2, HEADS_PER_STEP, t, t), BF16),
                        pltpu.VMEM((HEADS_PER_STEP, AUG_W, t), F32)],
        compiler_params=_cparams(("parallel", "parallel", "arbitrary")),
        name="fox_attention",
    )(q_aug, k_aug, vt, z)


def _pad_cols(w, n):
    return jnp.pad(w, ((0, 0), (0, n - w.shape[1])))


def _aug_weight(w, scale):
    d = w.shape[0]
    w = (w * scale).reshape(d, B_HEADS, B_DH)
    w = jnp.pad(w, ((0, 0), (0, 0), (0, AUG_W - B_DH)))
    return w.reshape(d, B_HEADS * AUG_W).astype(BF16)


def _placement(f_off, one_off, sign):
    place = np.zeros((LANES, B_HEADS * AUG_W), np.float32)
    ones = np.zeros((1, B_HEADS * AUG_W), np.float32)
    for p in range(N_FPARTS):
        for h in range(B_HEADS):
            place[p * B_HEADS + h, h * AUG_W + f_off + p] = sign
            ones[0, h * AUG_W + one_off + p] = 1.0
    return jnp.asarray(place, BF16), jnp.asarray(ones, F32)


def _vec3(m, n):
    return [t[:, None, :] for t in jnp.split(m, n, axis=-1)]


def kernel(x, c, a_mod_w, a_mod_b, a_pre_g, a_post_g, a_w_in, a_conv_w, a_a_log, a_dt_bias, a_o_gain, a_w_out, kv_mod_w, kv_mod_b, kv_norm_g, kv_w, kv_b_f, b_mod_w, b_mod_b, b_pre_g, b_post_g, b_w_in, b_w_out):
    place_q, ones_q = _placement(AUG_F0, AUG_F0 + N_FPARTS, 1.0)
    place_k, ones_k = _placement(AUG_F0 + N_FPARTS, AUG_F0, -1.0)

    for j in range(N_A_LAYERS):
        shift, scale, gate = _vec3(_adaln(c, a_mod_w[j], a_mod_b[j]), 3)
        w_in = a_w_in[j]
        n_main = A_CONV_CH + A_V
        w_main = w_in[:, :n_main].astype(BF16)
        w_ab = _pad_cols(w_in[:, n_main:], LANES).astype(BF16)
        qkv, z, ab = _a_in_proj(x, shift, scale, a_pre_g[j][None, :], w_main, w_ab)
        a_log_pad = _pad_cols(a_a_log[j][None, :], LANES)
        dt_pad = _pad_cols(a_dt_bias[j][None, :], LANES)
        og = _gdn_mixer(qkv, z, ab, a_conv_w[j], a_log_pad, dt_pad, a_o_gain[j][None, :])
        x = _out_proj(og, x, gate, a_post_g[j][None, :], a_w_out[j].astype(BF16))

    shift, scale = _vec3(_adaln(c, kv_mod_w, kv_mod_b), 2)
    wk_aug = _aug_weight(kv_w[:, :B_W], 1.0)
    wvt = _aug_weight(kv_w[:, B_W:2 * B_W], 1.0).T
    wf = _pad_cols(kv_w[:, 2 * B_W:], LANES).astype(BF16)
    bf = _pad_cols(kv_b_f[None, :], LANES)
    k_aug, vt, fcat = _kv_proj(x, shift, scale, kv_norm_g[None, :], wk_aug, wvt, wf, bf, place_k, ones_k)

    for j in range(N_B_LAYERS):
        shift, scale, gate = _vec3(_adaln(c, b_mod_w[j], b_mod_b[j]), 3)
        wq_aug = _aug_weight(b_w_in[j][:, :B_W], B_DH ** -0.5)
        wz = b_w_in[j][:, B_W:].astype(BF16)
        q_aug, z = _b_in_proj(x, shift, scale, b_pre_g[j][None, :], wq_aug, wz, fcat, place_q, ones_q)
        og = _fox_attention(q_aug, k_aug, vt, z)
        x = _out_proj(og, x, gate, b_post_g[j][None, :], b_w_out[j].astype(BF16))
    return x
```

```python
import functools

import jax
import jax.numpy as jnp
import numpy as np
from jax import lax
from jax.experimental import pallas as pl
from jax.experimental.pallas import tpu as pltpu

F32 = jnp.float32
BF16 = jnp.bfloat16

D_MODEL = 1024
N_A_LAYERS = 2
N_B_LAYERS = 2
A_HEADS = 8
A_DK = 128
A_DV = 128
A_QK = A_HEADS * A_DK
A_V = A_HEADS * A_DV
A_CONV = 4
A_CONV_CH = 2 * A_QK + A_V
B_HEADS = 16
B_DH = 64
B_W = B_HEADS * B_DH
NORM_EPS = 1e-6

LANES = 128
SUBLANES = 8
GDN_CHUNK = 128
GDN_TILE = 256
VMEM_LIMIT = 56 * 1024 * 1024

AUG_W = 128
V_ROWS = 80
AUG_F0 = B_DH
N_FPARTS = 3


def _cparams(sem):
    return pltpu.CompilerParams(dimension_semantics=sem, vmem_limit_bytes=VMEM_LIMIT)


def _silu(x):
    return x / (1.0 + jnp.exp(-x))


def _softplus(x):
    return jnp.maximum(x, 0.0) + jnp.log(1.0 + jnp.exp(-jnp.abs(x)))


def _dot(a, b):
    return jnp.dot(a.astype(BF16), b.astype(BF16), preferred_element_type=F32)


def _dot_nt(a, b):
    return lax.dot_general(a.astype(BF16), b.astype(BF16), (((1,), (1,)), ((), ())),
                           preferred_element_type=F32)


def _dot_tn(a, b):
    return lax.dot_general(a.astype(BF16), b.astype(BF16), (((0,), (0,)), ((), ())),
                           preferred_element_type=F32)


def _split2(a):
    hi = a.astype(BF16)
    lo = (a - hi.astype(F32)).astype(BF16)
    return hi, lo


def _dot3(a, b):
    ah, al = _split2(a)
    bh, bl = _split2(b)
    mm = functools.partial(jnp.dot, preferred_element_type=F32)
    return mm(ah, bh) + (mm(ah, bl) + mm(al, bh))


def _norm_mod(x, gain, scale, shift):
    ms = jnp.mean(x * x, axis=-1, keepdims=True)
    y = x * lax.rsqrt(ms + NORM_EPS) * gain
    return y * (1.0 + scale) + shift


def _mod_kernel(c_ref, w_ref, b_ref, o_ref):
    s = _silu(c_ref[...])
    o_ref[...] = jnp.dot(s, w_ref[...], precision=lax.Precision.HIGHEST,
                         preferred_element_type=F32) + b_ref[...]


def _adaln(c, w, b):
    bsz, d = c.shape
    n = w.shape[1]
    tn = 512
    return pl.pallas_call(
        _mod_kernel,
        grid=(n // tn,),
        in_specs=[pl.BlockSpec((bsz, d), lambda j: (0, 0)),
                  pl.BlockSpec((d, tn), lambda j: (0, j)),
                  pl.BlockSpec((1, tn), lambda j: (0, j))],
        out_specs=pl.BlockSpec((bsz, tn), lambda j: (0, j)),
        out_shape=jax.ShapeDtypeStruct((bsz, n), F32),
        compiler_params=_cparams(("arbitrary",)),
        name="adaln_mod",
    )(c, w, b.reshape(1, n))


A_IN_TM = 512
A_IN_TN = 512


def _a_in_kernel(x_ref, shift_ref, scale_ref, g_ref, w_ref, wab_ref, qkv_ref, z_ref, ab_ref):
    h = _norm_mod(x_ref[0], g_ref[...], scale_ref[0], shift_ref[0]).astype(BF16)
    for j in range(A_CONV_CH // A_IN_TN):
        sl = slice(j * A_IN_TN, (j + 1) * A_IN_TN)
        qkv_ref[0, :, sl] = jnp.dot(h, w_ref[:, sl], preferred_element_type=F32).astype(BF16)
    for j in range(A_V // A_IN_TN):
        sl = slice(j * A_IN_TN, (j + 1) * A_IN_TN)
        wsl = slice(A_CONV_CH + j * A_IN_TN, A_CONV_CH + (j + 1) * A_IN_TN)
        z_ref[0, :, sl] = jnp.dot(h, w_ref[:, wsl], preferred_element_type=F32).astype(BF16)
    ab_ref[0] = jnp.dot(h, wab_ref[...], preferred_element_type=F32)


def _a_in_proj(x, shift, scale, gain, w_main, w_ab):
    bsz, seq, d = x.shape
    tm = A_IN_TM
    row = lambda b, i: (b, i, 0)
    vec = lambda b, i: (b, 0, 0)
    const = lambda b, i: (0, 0)
    return pl.pallas_call(
        _a_in_kernel,
        grid=(bsz, seq // tm),
        in_specs=[pl.BlockSpec((1, tm, d), row),
                  pl.BlockSpec((1, 1, d), vec),
                  pl.BlockSpec((1, 1, d), vec),
                  pl.BlockSpec((1, d), const),
                  pl.BlockSpec(w_main.shape, const),
                  pl.BlockSpec(w_ab.shape, const)],
        out_specs=[pl.BlockSpec((1, tm, A_CONV_CH), row),
                   pl.BlockSpec((1, tm, A_V), row),
                   pl.BlockSpec((1, tm, LANES), row)],
        out_shape=[jax.ShapeDtypeStruct((bsz, seq, A_CONV_CH), BF16),
                   jax.ShapeDtypeStruct((bsz, seq, A_V), BF16),
                   jax.ShapeDtypeStruct((bsz, seq, LANES), F32)],
        compiler_params=_cparams(("parallel", "parallel")),
        name="a_in_proj",
    )(x, shift, scale, gain, w_main, w_ab)


def _inv_unit_lower(a_list, row, col):
    c = a_list[0].shape[0]
    eye = jnp.where(row == col, 1.0, 0.0).astype(F32)

    def blk(shift):
        return (row >> shift) == (col >> shift)

    b_prev = blk(3)
    ads = [jnp.where(b_prev, a, 0.0) for a in a_list]
    adb = [ad.astype(BF16) for ad in ads]
    a2s = [_dot(x, x) for x in adb]
    a2b = [a2.astype(BF16) for a2 in a2s]
    a3s = [_dot(x, y) for x, y in zip(adb, a2b)]
    a4s = [_dot(y, y) for y in a2b]
    p1s = [eye - ad + a2 - a3 for ad, a2, a3 in zip(ads, a2s, a3s)]
    ts = [p1 + _dot(p1, a4) for p1, a4 in zip(p1s, a4s)]
    shift = 4
    while (1 << shift) <= c:
        b_cur = blk(shift)
        sel = jnp.logical_and(b_cur, jnp.logical_not(b_prev))
        offs = [jnp.where(sel, a, 0.0).astype(BF16) for a in a_list]
        tbs = [t.astype(BF16) for t in ts]
        xs = [_dot(off, tb) for off, tb in zip(offs, tbs)]
        ts = [t - _dot(tb, x) for t, tb, x in zip(ts, tbs, xs)]
        b_prev = b_cur
        shift += 1
    rs = [eye - t - _dot3(a, t) for a, t in zip(a_list, ts)]
    return [t + _dot(t, r) for t, r in zip(ts, rs)]


def _gdn_kernel(qkv_ref, z_ref, ab_ref, convw_ref, alog_ref, dtb_ref, ogain_ref, o_ref,
                xe_ref, state_ref):
    c = GDN_CHUNK
    tile = GDN_TILE
    t_idx = pl.program_id(1)

    @pl.when(t_idx == 0)
    def _():
        xe_ref[0:SUBLANES, :] = jnp.zeros((SUBLANES, A_CONV_CH), F32)
        state_ref[...] = jnp.zeros_like(state_ref)

    x = qkv_ref[0].astype(F32)
    xe_ref[SUBLANES:SUBLANES + tile, :] = x
    y = convw_ref[A_CONV - 1:A_CONV, :] * x
    for k in range(A_CONV - 1):
        off = SUBLANES - (A_CONV - 1) + k
        y = y + convw_ref[k:k + 1, :] * xe_ref[off:off + tile, :]
    xe_ref[0:SUBLANES, :] = x[tile - SUBLANES:tile, :]
    y = _silu(y)

    ab = ab_ref[0]
    g = -jnp.exp(alog_ref[...]) * _softplus(ab + dtb_ref[...])
    beta = 1.0 / (1.0 + jnp.exp(-ab))
    row = lax.broadcasted_iota(jnp.int32, (c, c), 0)
    col = lax.broadcasted_iota(jnp.int32, (c, c), 1)
    tril = jnp.where(row >= col, 1.0, 0.0).astype(F32)
    lower = row >= col
    strict = row > col

    chunks = range(tile // c)
    heads = range(A_HEADS)
    probs = [(ci, h) for ci in chunks for h in heads]
    gcs, egs, e_lasts, e_rests = [], [], [], []
    for ci in chunks:
        gc = jnp.dot(tril, g[ci * c:(ci + 1) * c], precision=lax.Precision.HIGHEST,
                     preferred_element_type=F32)
        g_last = gc[c - 1:c, :]
        gcs.append((gc, gc.T))
        egs.append(jnp.exp(gc))
        e_lasts.append(jnp.exp(g_last))
        e_rests.append(jnp.exp(g_last - gc))
    qs, ks, kbs, rhss, decs = [], [], [], [], []
    for ci, h in probs:
        rows = slice(ci * c, (ci + 1) * c)
        q = y[rows, h * A_DK:(h + 1) * A_DK]
        k = y[rows, A_QK + h * A_DK:A_QK + (h + 1) * A_DK]
        v = y[rows, 2 * A_QK + h * A_DV:2 * A_QK + (h + 1) * A_DV]
        q = q * (lax.rsqrt(jnp.sum(q * q, axis=-1, keepdims=True) + NORM_EPS) * (A_DK ** -0.5))
        k = k * lax.rsqrt(jnp.sum(k * k, axis=-1, keepdims=True) + NORM_EPS)
        b_col = beta[rows, A_HEADS + h:A_HEADS + h + 1]
        kb = k * b_col
        gc, gc_t = gcs[ci]
        diff = gc[:, h:h + 1] - gc_t[h:h + 1, :]
        decs.append(jnp.exp(jnp.where(lower, diff, -jnp.inf)))
        qs.append(q)
        ks.append(k)
        kbs.append(kb)
        rhss.append(jnp.concatenate([v * b_col, kb * egs[ci][:, h:h + 1]], axis=1))
    ms = [_dot_nt(jnp.concatenate([kb, q], axis=0), k) for kb, q, k in zip(kbs, qs, ks)]
    a_list = [jnp.where(strict, m[:c] * dec, 0.0) for m, dec in zip(ms, decs)]
    qks = [m[c:] * dec for m, dec in zip(ms, decs)]
    t_invs = _inv_unit_lower(a_list, row, col)
    sols = [_dot(t, rhs) for t, rhs in zip(t_invs, rhss)]
    states = [state_ref[h] for h in heads]
    outs = []
    for ci in chunks:
        pr = [ci * A_HEADS + h for h in heads]
        wqs = [_dot(jnp.concatenate([sols[p][:, A_DV:], qs[p] * egs[ci][:, h:h + 1]], axis=0),
                    states[h]) for h, p in zip(heads, pr)]
        v_news = [sols[p][:, :A_DV] - wq[:c] for p, wq in zip(pr, wqs)]
        outs += [wq[c:] + _dot(qks[p], vn) for p, wq, vn in zip(pr, wqs, v_news)]
        upds = [_dot_tn(ks[p] * e_rests[ci][:, h:h + 1], vn) for h, p, vn in zip(heads, pr, v_news)]
        states = [states[h] * e_lasts[ci][:, h:h + 1] + upds[h] for h in heads]
    for h in heads:
        state_ref[h] = states[h]
    for (ci, h), o in zip(probs, outs):
        rows = slice(ci * c, (ci + 1) * c)
        sl = slice(h * A_DV, (h + 1) * A_DV)
        o = o * lax.rsqrt(jnp.mean(o * o, axis=-1, keepdims=True) + NORM_EPS) * ogain_ref[...]
        zz = z_ref[0, rows, sl].astype(F32)
        o_ref[0, rows, sl] = (o * _silu(zz)).astype(BF16)


def _gdn_mixer(qkv, z, ab, conv_w, a_log_pad, dt_bias_pad, o_gain):
    bsz, seq, _ = qkv.shape
    c = GDN_TILE
    row = lambda b, i: (b, i, 0)
    const = lambda b, i: (0, 0)
    return pl.pallas_call(
        _gdn_kernel,
        grid=(bsz, seq // c),
        in_specs=[pl.BlockSpec((1, c, A_CONV_CH), row),
                  pl.BlockSpec((1, c, A_V), row),
                  pl.BlockSpec((1, c, LANES), row),
                  pl.BlockSpec((A_CONV, A_CONV_CH), const),
                  pl.BlockSpec((1, LANES), const),
                  pl.BlockSpec((1, LANES), const),
                  pl.BlockSpec((1, A_DV), const)],
        out_specs=pl.BlockSpec((1, c, A_V), row),
        out_shape=jax.ShapeDtypeStruct((bsz, seq, A_V), BF16),
        scratch_shapes=[pltpu.VMEM((SUBLANES + c, A_CONV_CH), F32),
                        pltpu.VMEM((A_HEADS, A_DK, A_DV), F32)],
        compiler_params=_cparams(("parallel", "arbitrary")),
        name="gdn_mixer",
    )(qkv, z, ab, conv_w, a_log_pad, dt_bias_pad, o_gain)


OUT_TM = 512


def _out_kernel(og_ref, x_ref, gate_ref, g_ref, w_ref, o_ref):
    y = jnp.dot(og_ref[0], w_ref[...], preferred_element_type=F32)
    yn = y * lax.rsqrt(jnp.mean(y * y, axis=-1, keepdims=True) + NORM_EPS) * g_ref[...]
    o_ref[0] = x_ref[0] + gate_ref[0] * yn


def _out_proj(og, x, gate, gain, w):
    bsz, seq, d = x.shape
    tm = OUT_TM
    row = lambda b, i: (b, i, 0)
    vec = lambda b, i: (b, 0, 0)
    const = lambda b, i: (0, 0)
    return pl.pallas_call(
        _out_kernel,
        grid=(bsz, seq // tm),
        in_specs=[pl.BlockSpec((1, tm, og.shape[-1]), row),
                  pl.BlockSpec((1, tm, d), row),
                  pl.BlockSpec((1, 1, d), vec),
                  pl.BlockSpec((1, d), const),
                  pl.BlockSpec(w.shape, const)],
        out_specs=pl.BlockSpec((1, tm, d), row),
        out_shape=jax.ShapeDtypeStruct((bsz, seq, d), F32),
        compiler_params=_cparams(("parallel", "parallel")),
        name="out_proj",
    )(og, x, gate, gain, w)


KV_TM = 512
KV_HEAD_GROUP = 4
ATT_T = 512


def _fparts(f):
    hi = f.astype(BF16).astype(F32)
    r1 = f - hi
    mid = r1.astype(BF16).astype(F32)
    lo = (r1 - mid).astype(BF16).astype(F32)
    lane = lax.broadcasted_iota(jnp.int32, f.shape, 1)
    out = jnp.where(lane < B_HEADS, hi, 0.0)
    out = out + jnp.where(jnp.logical_and(lane >= B_HEADS, lane < 2 * B_HEADS),
                          pltpu.roll(mid, B_HEADS, 1), 0.0)
    out = out + jnp.where(jnp.logical_and(lane >= 2 * B_HEADS, lane < 3 * B_HEADS),
                          pltpu.roll(lo, 2 * B_HEADS, 1), 0.0)
    return out


def _kv_kernel(x_ref, shift_ref, scale_ref, g_ref, wk_ref, wvt_ref, wf_ref, bf_ref, place_ref,
               ones_ref, k_ref, vt_ref, fcat_ref, carry_ref):
    tm = x_ref.shape[1]

    @pl.when(pl.program_id(1) == 0)
    def _():
        carry_ref[...] = jnp.zeros_like(carry_ref)

    h = _norm_mod(x_ref[0], g_ref[...], scale_ref[0], shift_ref[0]).astype(BF16)
    for grp in range(B_HEADS // KV_HEAD_GROUP):
        rows = slice(grp * KV_HEAD_GROUP * V_ROWS, (grp + 1) * KV_HEAD_GROUP * V_ROWS)
        vt = lax.dot_general(wvt_ref[rows, :], h, (((1,), (1,)), ((), ())),
                             preferred_element_type=F32)
        r = lax.broadcasted_iota(jnp.int32, vt.shape, 0)
        for hh in range(KV_HEAD_GROUP):
            vt = jnp.where(r == hh * V_ROWS + B_DH, 1.0, vt)
        vt = vt.astype(BF16)
        for hh in range(KV_HEAD_GROUP):
            for jb in range(tm // ATT_T):
                vt_ref[0, grp * KV_HEAD_GROUP + hh, jb] = vt[hh * V_ROWS:(hh + 1) * V_ROWS,
                                                             jb * ATT_T:(jb + 1) * ATT_T]
    f =jnp.dot(h, wf_ref[...], preferred_element_type=F32) + bf_ref[...]
    lane = lax.broadcasted_iota(jnp.int32, f.shape, 1)
    log_f = jnp.where(lane < B_HEADS, -_softplus(-f), 0.0)
    row = lax.broadcasted_iota(jnp.int32, (tm, tm), 0)
    col = lax.broadcasted_iota(jnp.int32, (tm, tm), 1)
    tril = jnp.where(row >= col, 1.0, 0.0).astype(F32)
    f_cum = jnp.dot(tril, log_f, precision=lax.Precision.HIGHEST,
                    preferred_element_type=F32) + carry_ref[...]
    carry_ref[...] = f_cum[tm - 1:tm, :]
    fcat = _fparts(f_cum).astype(BF16)
    fcat_ref[0] = fcat
    for j in range(k_ref.shape[2] // 512):
        sl = slice(j * 512, (j + 1) * 512)
        k_aug = (jnp.dot(h, wk_ref[:, sl], preferred_element_type=F32)
                 + jnp.dot(fcat, place_ref[:, sl], preferred_element_type=F32) + ones_ref[:, sl])
        k_ref[0, :, sl] = k_aug.astype(BF16)


def _kv_proj(x, shift, scale, gain, wk_aug, wvt, wf, bf, place_k, ones_k):
    bsz, seq, d = x.shape
    tm = KV_TM
    row = lambda b, i: (b, i, 0)
    vec = lambda b, i: (b, 0, 0)
    const = lambda b, i: (0, 0)
    kw = B_HEADS * AUG_W
    nblk = tm // ATT_T
    return pl.pallas_call(
        _kv_kernel,
        grid=(bsz, seq // tm),
        in_specs=[pl.BlockSpec((1, tm, d), row),
                  pl.BlockSpec((1, 1, d), vec),
                  pl.BlockSpec((1, 1, d), vec),
                  pl.BlockSpec((1, d), const),
                  pl.BlockSpec(wk_aug.shape, const),
                  pl.BlockSpec(wvt.shape, const),
                  pl.BlockSpec(wf.shape, const),
                  pl.BlockSpec((1, LANES), const),
                  pl.BlockSpec(place_k.shape, const),
                  pl.BlockSpec((1, kw), const)],
        out_specs=[pl.BlockSpec((1, tm, kw), row),
                   pl.BlockSpec((1, B_HEADS, nblk, V_ROWS, ATT_T), lambda b, i: (b, 0, i, 0, 0)),
                   pl.BlockSpec((1, tm, LANES), row)],
        out_shape=[jax.ShapeDtypeStruct((bsz, seq, kw), BF16),
                   jax.ShapeDtypeStruct((bsz, B_HEADS, seq // ATT_T, V_ROWS, ATT_T), BF16),
                   jax.ShapeDtypeStruct((bsz, seq, LANES), BF16)],
        scratch_shapes=[pltpu.VMEM((1, LANES), F32)],
        compiler_params=_cparams(("parallel", "arbitrary")),
        name="kv_proj",
    )(x, shift, scale, gain, wk_aug, wvt, wf, bf, place_k, ones_k)


B_IN_TM = 512


def _b_in_kernel(x_ref, shift_ref, scale_ref, g_ref, wq_ref, wz_ref, fcat_ref, place_ref, ones_ref,
                 q_ref, z_ref):
    h = _norm_mod(x_ref[0], g_ref[...], scale_ref[0], shift_ref[0]).astype(BF16)
    fcat = fcat_ref[0]
    for j in range(q_ref.shape[2] // 512):
        sl = slice(j * 512, (j + 1) * 512)
        q_aug = (jnp.dot(h, wq_ref[:, sl], preferred_element_type=F32)
                 + jnp.dot(fcat, place_ref[:, sl], preferred_element_type=F32) + ones_ref[:, sl])
        q_ref[0, :, sl] = q_aug.astype(BF16)
    for j in range(z_ref.shape[2] // 512):
        sl = slice(j * 512, (j + 1) * 512)
        z_ref[0, :, sl] = jnp.dot(h, wz_ref[:, sl], preferred_element_type=F32).astype(BF16)


def _b_in_proj(x, shift, scale, gain, wq_aug, wz, fcat, place_q, ones_q):
    bsz, seq, d = x.shape
    tm = B_IN_TM
    row = lambda b, i: (b, i, 0)
    vec = lambda b, i: (b, 0, 0)
    const = lambda b, i: (0, 0)
    qw = B_HEADS * AUG_W
    return pl.pallas_call(
        _b_in_kernel,
        grid=(bsz, seq // tm),
        in_specs=[pl.BlockSpec((1, tm, d), row),
                  pl.BlockSpec((1, 1, d), vec),
                  pl.BlockSpec((1, 1, d), vec),
                  pl.BlockSpec((1, d), const),
                  pl.BlockSpec(wq_aug.shape, const),
                  pl.BlockSpec(wz.shape, const),
                  pl.BlockSpec((1, tm, LANES), row),
                  pl.BlockSpec(place_q.shape, const),
                  pl.BlockSpec((1, qw), const)],
        out_specs=[pl.BlockSpec((1, tm, qw), row),
                   pl.BlockSpec((1, tm, B_W), row)],
        out_shape=[jax.ShapeDtypeStruct((bsz, seq, qw), BF16),
                   jax.ShapeDtypeStruct((bsz, seq, B_W), BF16)],
        compiler_params=_cparams(("parallel", "parallel")),
        name="b_in_proj",
    )(x, shift, scale, gain, wq_aug, wz, fcat, place_q, ones_q)


HEADS_PER_STEP = 4


def _attn_kernel(q_ref, k_ref, vt_ref, z_ref, o_ref, s_ref, p_ref, acc_ref):
    t = ATT_T
    i = pl.program_id(2)
    qs = [q_ref[0, :, hh * AUG_W:(hh + 1) * AUG_W] for hh in range(HEADS_PER_STEP)]

    hds = range(HEADS_PER_STEP)

    def scores(jb):
        start = pl.multiple_of(jb * t, t)
        return [lax.dot_general(k_ref[0, pl.ds(start, t), hh * AUG_W:(hh + 1) * AUG_W], qs[hh],
                                (((1,), (1,)), ((), ())), preferred_element_type=F32) for hh in hds]

    def weighted_values(jb, slot):
        return [jnp.dot(vt_ref[0, hh, jb], p_ref[slot, hh], preferred_element_type=F32)
                for hh in hds]

    def softmax_update(ms, masked, slot):
        ss = [s_ref[hh] for hh in hds]
        if masked:
            key = lax.broadcasted_iota(jnp.int32, (t, t), 0)
            qry = lax.broadcasted_iota(jnp.int32, (t, t), 1)
            ss = [jnp.where(key <= qry, s, -jnp.inf) for s in ss]
        m_news = [jnp.maximum(ms[hh], jnp.max(ss[hh], axis=0, keepdims=True)) for hh in hds]
        for hh in hds:
            p_ref[slot, hh] = jnp.exp(ss[hh] - m_news[hh]).astype(BF16)
        return m_news, [jnp.exp(ms[hh] - m_news[hh]) for hh in hds]

    def accumulate(alphas, pvs):
        for hh in hds:
            acc_ref[hh] = alphas[hh] * acc_ref[hh] + pvs[hh]

    def step(j, slot, carry):
        ms, alphas = carry
        pvs = weighted_values(jnp.maximum(j - 1, 0), 1 - slot)
        ss_next = scores(j + 1)
        ms, alphas_new = softmax_update(ms, False, slot)
        accumulate(alphas, pvs)
        for hh in hds:
            s_ref[hh] = ss_next[hh]
        return ms, alphas_new

    def diagonal(slot, carry):
        ms, alphas = carry
        pvs = weighted_values(jnp.maximum(i - 1, 0), 1 - slot)
        _, alphas_new = softmax_update(ms, True, slot)
        accumulate(alphas, pvs)
        accumulate(alphas_new, weighted_values(i, slot))

    ss0 = scores(0)
    for hh in hds:
        s_ref[hh] = ss0[hh]
        p_ref[1, hh] = jnp.zeros((t, t), BF16)
        acc_ref[hh] = jnp.zeros((V_ROWS, t), F32)
    init = ([jnp.full((1, t), -jnp.inf, F32) for _ in hds], [jnp.ones((1, t), F32) for _ in hds])
    carry = lax.fori_loop(0, i // 2, lambda jj, cr: step(2 * jj + 1, 1, step(2 * jj, 0, cr)), init)

    @pl.when(i % 2 == 0)
    def _():
        diagonal(0, carry)

    @pl.when(i % 2 == 1)
    def _():
        diagonal(1, step(i - 1, 0, carry))

    o_t = jnp.concatenate([acc_ref[hh][:B_DH, :] / acc_ref[hh][B_DH:B_DH + 1, :] for hh in hds],
                          axis=0)
    o_ref[0] = (o_t.T * _silu(z_ref[0].astype(F32))).astype(BF16)


def _fox_attention(q_aug, k_aug, vt, z):
    bsz, seq, _ = z.shape
    t = ATT_T
    n_pairs = B_HEADS // HEADS_PER_STEP
    pw = HEADS_PER_STEP * AUG_W
    return pl.pallas_call(
        _attn_kernel,
        grid=(bsz, n_pairs, seq // t),
        in_specs=[pl.BlockSpec((1, t, pw), lambda b, p, i: (b, i, p)),
                  pl.BlockSpec((1, seq, pw), lambda b, p, i: (b, 0, p)),
                  pl.BlockSpec((1, HEADS_PER_STEP, seq // t, V_ROWS, t),
                               lambda b, p, i: (b, p, 0, 0, 0)),
                  pl.BlockSpec((1, t, HEADS_PER_STEP * B_DH), lambda b, p, i: (b, i, p))],
        out_specs=pl.BlockSpec((1, t, HEADS_PER_STEP * B_DH), lambda b, p, i: (b, i, p)),
        out_shape=jax.ShapeDtypeStruct((bsz, seq, B_W), BF16),
        scratch_shapes=[pltpu.VMEM((HEADS_PER_STEP, t, t), F32),
                        pltpu.VMEM((2, HEADS_PER_STEP, t, t), BF16),
                        pltpu.VMEM((HEADS_PER_STEP, V_ROWS, t), F32)],
        compiler_params=_cparams(("parallel", "parallel", "arbitrary")),
        name="fox_attention",
    )(q_aug, k_aug, vt, z)


def _pad_cols(w, n):
    return jnp.pad(w, ((0, 0), (0, n - w.shape[1])))


def _aug_weight(w, scale, width=AUG_W):
    d = w.shape[0]
    w = (w * scale).reshape(d, B_HEADS, B_DH)
    w = jnp.pad(w, ((0, 0), (0, 0), (0, width - B_DH)))
    return w.reshape(d, B_HEADS * width).astype(BF16)


def _placement(f_off, one_off, sign):
    place = np.zeros((LANES, B_HEADS * AUG_W), np.float32)
    ones = np.zeros((1, B_HEADS * AUG_W), np.float32)
    for p in range(N_FPARTS):
        for h in range(B_HEADS):
            place[p * B_HEADS + h, h * AUG_W + f_off + p] = sign
            ones[0, h * AUG_W + one_off + p] = 1.0
    return jnp.asarray(place, BF16), jnp.asarray(ones, F32)


def _vec3(m, n):
    return [t[:, None, :] for t in jnp.split(m, n, axis=-1)]


def kernel(x, c, a_mod_w, a_mod_b, a_pre_g, a_post_g, a_w_in, a_conv_w, a_a_log, a_dt_bias, a_o_gain, a_w_out, kv_mod_w, kv_mod_b, kv_norm_g, kv_w, kv_b_f, b_mod_w, b_mod_b, b_pre_g, b_post_g, b_w_in, b_w_out):
    place_q, ones_q = _placement(AUG_F0, AUG_F0 + N_FPARTS, 1.0)
    place_k, ones_k = _placement(AUG_F0 + N_FPARTS, AUG_F0, -1.0)

    for j in range(N_A_LAYERS):
        shift, scale, gate = _vec3(_adaln(c, a_mod_w[j], a_mod_b[j]), 3)
        w_in = a_w_in[j]
        n_main = A_CONV_CH + A_V
        w_main = w_in[:, :n_main].astype(BF16)
        w_ab = _pad_cols(w_in[:, n_main:], LANES).astype(BF16)
        qkv, z, ab = _a_in_proj(x, shift, scale, a_pre_g[j][None, :], w_main, w_ab)
        a_log_pad = _pad_cols(a_a_log[j][None, :], LANES)
        dt_pad = _pad_cols(a_dt_bias[j][None, :], LANES)
        og = _gdn_mixer(qkv, z, ab, a_conv_w[j], a_log_pad, dt_pad, a_o_gain[j][None, :])
        x = _out_proj(og, x, gate, a_post_g[j][None, :], a_w_out[j].astype(BF16))

    shift, scale = _vec3(_adaln(c, kv_mod_w, kv_mod_b), 2)
    wk_aug = _aug_weight(kv_w[:, :B_W], 1.0)
    wvt = _aug_weight(kv_w[:, B_W:2 * B_W], 1.0, V_ROWS).T
    wf = _pad_cols(kv_w[:, 2 * B_W:], LANES).astype(BF16)
    bf = _pad_cols(kv_b_f[None, :], LANES)
    k_aug, vt, fcat = _kv_proj(x, shift, scale, kv_norm_g[None, :], wk_aug, wvt, wf, bf, place_k, ones_k)

    for j in range(N_B_LAYERS):
        shift, scale, gate = _vec3(_adaln(c, b_mod_w[j], b_mod_b[j]), 3)
        wq_aug = _aug_weight(b_w_in[j][:, :B_W], B_DH ** -0.5)
        wz = b_w_in[j][:, B_W:].astype(BF16)
        q_aug, z = _b_in_proj(x, shift, scale, b_pre_g[j][None, :], wq_aug, wz, fcat, place_q, ones_q)
        og = _fox_attention(q_aug, k_aug, vt, z)
        x = _out_proj(og, x, gate, b_post_g[j][None, :], b_w_out[j].astype(BF16))
    return x
```

```python
import functools

import jax
import jax.numpy as jnp
import numpy as np
from jax import lax
from jax.experimental import pallas as pl
from jax.experimental.pallas import tpu as pltpu

F32 = jnp.float32
BF16 = jnp.bfloat16

D_MODEL = 1024
N_A_LAYERS = 2
N_B_LAYERS = 2
A_HEADS = 8
A_DK = 128
A_DV = 128
A_QK = A_HEADS * A_DK
A_V = A_HEADS * A_DV
A_CONV = 4
A_CONV_CH = 2 * A_QK + A_V
B_HEADS = 16
B_DH = 64
B_W = B_HEADS * B_DH
NORM_EPS = 1e-6

LANES = 128
SUBLANES = 8
GDN_CHUNK = 128
GDN_TILE = 512
GDN_GROUP = 2
VMEM_LIMIT = 56 * 1024 * 1024

AUG_W = 128
V_ROWS = 80
AUG_F0 = B_DH
N_FPARTS = 3


def _cparams(sem):
    return pltpu.CompilerParams(dimension_semantics=sem, vmem_limit_bytes=VMEM_LIMIT)


def _silu(x):
    return x / (1.0 + jnp.exp(-x))


def _softplus(x):
    return jnp.maximum(x, 0.0) + jnp.log(1.0 + jnp.exp(-jnp.abs(x)))


def _dot(a, b):
    return jnp.dot(a.astype(BF16), b.astype(BF16), preferred_element_type=F32)


def _dot_nt(a, b):
    return lax.dot_general(a.astype(BF16), b.astype(BF16), (((1,), (1,)), ((), ())),
                           preferred_element_type=F32)


def _dot_tn(a, b):
    return lax.dot_general(a.astype(BF16), b.astype(BF16), (((0,), (0,)), ((), ())),
                           preferred_element_type=F32)


def _split2(a):
    hi = a.astype(BF16)
    lo = (a - hi.astype(F32)).astype(BF16)
    return hi, lo


def _dot3(a, b):
    ah, al = _split2(a)
    bh, bl = _split2(b)
    mm = functools.partial(jnp.dot, preferred_element_type=F32)
    return mm(ah, bh) + (mm(ah, bl) + mm(al, bh))


def _norm_mod(x, gain, scale, shift):
    ms = jnp.mean(x * x, axis=-1, keepdims=True)
    y = x * lax.rsqrt(ms + NORM_EPS) * gain
    return y * (1.0 + scale) + shift


def _mod_kernel(c_ref, w_ref, b_ref, o_ref):
    s = _silu(c_ref[...])
    o_ref[...] = jnp.dot(s, w_ref[...], precision=lax.Precision.HIGHEST,
                         preferred_element_type=F32) + b_ref[...]


def _adaln(c, w, b):
    bsz, d = c.shape
    n = w.shape[1]
    tn = 512
    return pl.pallas_call(
        _mod_kernel,
        grid=(n // tn,),
        in_specs=[pl.BlockSpec((bsz, d), lambda j: (0, 0)),
                  pl.BlockSpec((d, tn), lambda j: (0, j)),
                  pl.BlockSpec((1, tn), lambda j: (0, j))],
        out_specs=pl.BlockSpec((bsz, tn), lambda j: (0, j)),
        out_shape=jax.ShapeDtypeStruct((bsz, n), F32),
        compiler_params=_cparams(("arbitrary",)),
        name="adaln_mod",
    )(c, w, b.reshape(1, n))


A_IN_TM = 512
A_IN_TN = 512


def _a_in_kernel(x_ref, shift_ref, scale_ref, g_ref, w_ref, wab_ref, qkv_ref, z_ref, ab_ref):
    h = _norm_mod(x_ref[0], g_ref[...], scale_ref[0], shift_ref[0]).astype(BF16)
    for j in range(A_CONV_CH // A_IN_TN):
        sl = slice(j * A_IN_TN, (j + 1) * A_IN_TN)
        qkv_ref[0, :, sl] = jnp.dot(h, w_ref[:, sl], preferred_element_type=F32).astype(BF16)
    for j in range(A_V // A_IN_TN):
        sl = slice(j * A_IN_TN, (j + 1) * A_IN_TN)
        wsl = slice(A_CONV_CH + j * A_IN_TN, A_CONV_CH + (j + 1) * A_IN_TN)
        z_ref[0, :, sl] = jnp.dot(h, w_ref[:, wsl], preferred_element_type=F32).astype(BF16)
    ab_ref[0] = jnp.dot(h, wab_ref[...], preferred_element_type=F32)


def _a_in_proj(x, shift, scale, gain, w_main, w_ab):
    bsz, seq, d = x.shape
    tm = A_IN_TM
    row = lambda b, i: (b, i, 0)
    vec = lambda b, i: (b, 0, 0)
    const = lambda b, i: (0, 0)
    return pl.pallas_call(
        _a_in_kernel,
        grid=(bsz, seq // tm),
        in_specs=[pl.BlockSpec((1, tm, d), row),
                  pl.BlockSpec((1, 1, d), vec),
                  pl.BlockSpec((1, 1, d), vec),
                  pl.BlockSpec((1, d), const),
                  pl.BlockSpec(w_main.shape, const),
                  pl.BlockSpec(w_ab.shape, const)],
        out_specs=[pl.BlockSpec((1, tm, A_CONV_CH), row),
                   pl.BlockSpec((1, tm, A_V), row),
                   pl.BlockSpec((1, tm, LANES), row)],
        out_shape=[jax.ShapeDtypeStruct((bsz, seq, A_CONV_CH), BF16),
                   jax.ShapeDtypeStruct((bsz, seq, A_V), BF16),
                   jax.ShapeDtypeStruct((bsz, seq, LANES), F32)],
        compiler_params=_cparams(("parallel", "parallel")),
        name="a_in_proj",
    )(x, shift, scale, gain, w_main, w_ab)


def _inv_unit_lower(a_list, row, col):
    c = a_list[0].shape[0]
    eye = jnp.where(row == col, 1.0, 0.0).astype(F32)

    def blk(shift):
        return (row >> shift) == (col >> shift)

    b_prev = blk(3)
    ads = [jnp.where(b_prev, a, 0.0) for a in a_list]
    adb = [ad.astype(BF16) for ad in ads]
    a2s = [_dot(x, x) for x in adb]
    a2b = [a2.astype(BF16) for a2 in a2s]
    a3s = [_dot(x, y) for x, y in zip(adb, a2b)]
    a4s = [_dot(y, y) for y in a2b]
    p1s = [eye - ad + a2 - a3 for ad, a2, a3 in zip(ads, a2s, a3s)]
    ts = [p1 + _dot(p1, a4) for p1, a4 in zip(p1s, a4s)]
    shift = 4
    while (1 << shift) <= c:
        b_cur = blk(shift)
        sel = jnp.logical_and(b_cur, jnp.logical_not(b_prev))
        offs = [jnp.where(sel, a, 0.0).astype(BF16) for a in a_list]
        tbs = [t.astype(BF16) for t in ts]
        xs = [_dot(off, tb) for off, tb in zip(offs, tbs)]
        ts = [t - _dot(tb, x) for t, tb, x in zip(ts, tbs, xs)]
        b_prev = b_cur
        shift += 1
    rs = [eye - t - _dot3(a, t) for a, t in zip(a_list, ts)]
    return [t + _dot(t, r) for t, r in zip(ts, rs)]


def _gdn_kernel(qkv_ref, z_ref, ab_ref, convw_ref, alog_ref, dtb_ref, ogain_ref, o_ref,
                xe_ref, state_ref):
    c = GDN_CHUNK
    tile = GDN_TILE
    t_idx = pl.program_id(1)

    @pl.when(t_idx == 0)
    def _():
        xe_ref[0:SUBLANES, :] = jnp.zeros((SUBLANES, A_CONV_CH), F32)
        state_ref[...] = jnp.zeros_like(state_ref)

    x = qkv_ref[0].astype(F32)
    xe_ref[SUBLANES:SUBLANES + tile, :] = x
    y = convw_ref[A_CONV - 1:A_CONV, :] * x
    for k in range(A_CONV - 1):
        off = SUBLANES - (A_CONV - 1) + k
        y = y + convw_ref[k:k + 1, :] * xe_ref[off:off + tile, :]
    xe_ref[0:SUBLANES, :] = x[tile - SUBLANES:tile, :]
    y = _silu(y)

    ab = ab_ref[0]
    g = -jnp.exp(alog_ref[...]) * _softplus(ab + dtb_ref[...])
    beta = 1.0 / (1.0 + jnp.exp(-ab))
    row = lax.broadcasted_iota(jnp.int32, (c, c), 0)
    col = lax.broadcasted_iota(jnp.int32, (c, c), 1)
    tril = jnp.where(row >= col, 1.0, 0.0).astype(F32)
    lower = row >= col
    strict = row > col

    chunks = range(tile // c)
    heads = range(A_HEADS)
    probs = [(ci, h) for ci in chunks for h in heads]
    gcs, egs, e_lasts, e_rests = [], [], [], []
    for ci in chunks:
        gc = jnp.dot(tril, g[ci * c:(ci + 1) * c], precision=lax.Precision.HIGHEST,
                     preferred_element_type=F32)
        g_last = gc[c - 1:c, :]
        gcs.append((gc, gc.T))
        egs.append(jnp.exp(gc))
        e_lasts.append(jnp.exp(g_last))
        e_rests.append(jnp.exp(g_last - gc))
    qs, ks, qks, sols = [], [], [], []
    for g0 in range(0, len(probs), GDN_GROUP * A_HEADS):
        grp = probs[g0:g0 + GDN_GROUP * A_HEADS]
        gq, gk, kbs, rhss, decs = [], [], [], [], []
        for ci, h in grp:
            rows = slice(ci * c, (ci + 1) * c)
            q = y[rows, h * A_DK:(h + 1) * A_DK]
            k = y[rows, A_QK + h * A_DK:A_QK + (h + 1) * A_DK]
            v = y[rows, 2 * A_QK + h * A_DV:2 * A_QK + (h + 1) * A_DV]
            q = q * (lax.rsqrt(jnp.sum(q * q, axis=-1, keepdims=True) + NORM_EPS) * (A_DK ** -0.5))
            k = k * lax.rsqrt(jnp.sum(k * k, axis=-1, keepdims=True) + NORM_EPS)
            b_col = beta[rows, A_HEADS + h:A_HEADS + h + 1]
            kb = k * b_col
            gc, gc_t = gcs[ci]
            diff = gc[:, h:h + 1] - gc_t[h:h + 1, :]
            decs.append(jnp.exp(jnp.where(lower, diff, -jnp.inf)))
            gq.append(q)
            gk.append(k)
            kbs.append(kb)
            rhss.append(jnp.concatenate([v * b_col, kb * egs[ci][:, h:h + 1]], axis=1))
        ms = [_dot_nt(jnp.concatenate([kb, q], axis=0), k) for kb, q, k in zip(kbs, gq, gk)]
        a_list = [jnp.where(strict, m[:c] * dec, 0.0) for m, dec in zip(ms, decs)]
        qks += [m[c:] * dec for m, dec in zip(ms, decs)]
        t_invs = _inv_unit_lower(a_list, row, col)
        sols += [_dot(t, rhs) for t, rhs in zip(t_invs, rhss)]
        qs += gq
        ks += gk
    states = [state_ref[h] for h in heads]
    outs = []
    for ci in chunks:
        pr = [ci * A_HEADS + h for h in heads]
        wqs = [_dot(jnp.concatenate([sols[p][:, A_DV:], qs[p] * egs[ci][:, h:h + 1]], axis=0),
                    states[h]) for h, p in zip(heads, pr)]
        v_news = [sols[p][:, :A_DV] - wq[:c] for p, wq in zip(pr, wqs)]
        outs += [wq[c:] + _dot(qks[p], vn) for p, wq, vn in zip(pr, wqs, v_news)]
        upds = [_dot_tn(ks[p] * e_rests[ci][:, h:h + 1], vn) for h, p, vn in zip(heads, pr, v_news)]
        states = [states[h] * e_lasts[ci][:, h:h + 1] + upds[h] for h in heads]
    for h in heads:
        state_ref[h] = states[h]
    for (ci, h), o in zip(probs, outs):
        rows = slice(ci * c, (ci + 1) * c)
        sl = slice(h * A_DV, (h + 1) * A_DV)
        o = o * lax.rsqrt(jnp.mean(o * o, axis=-1, keepdims=True) + NORM_EPS) * ogain_ref[...]
        zz = z_ref[0, rows, sl].astype(F32)
        o_ref[0, rows, sl] = (o * _silu(zz)).astype(BF16)


def _gdn_mixer(qkv, z, ab, conv_w, a_log_pad, dt_bias_pad, o_gain):
    bsz, seq, _ = qkv.shape
    c = GDN_TILE
    row = lambda b, i: (b, i, 0)
    const = lambda b, i: (0, 0)
    return pl.pallas_call(
        _gdn_kernel,
        grid=(bsz, seq // c),
        in_specs=[pl.BlockSpec((1, c, A_CONV_CH), row),
                  pl.BlockSpec((1, c, A_V), row),
                  pl.BlockSpec((1, c, LANES), row),
                  pl.BlockSpec((A_CONV, A_CONV_CH), const),
                  pl.BlockSpec((1, LANES), const),
                  pl.BlockSpec((1, LANES), const),
                  pl.BlockSpec((1, A_DV), const)],
        out_specs=pl.BlockSpec((1, c, A_V), row),
        out_shape=jax.ShapeDtypeStruct((bsz, seq, A_V), BF16),
        scratch_shapes=[pltpu.VMEM((SUBLANES + c, A_CONV_CH), F32),
                        pltpu.VMEM((A_HEADS, A_DK, A_DV), F32)],
        compiler_params=_cparams(("parallel", "arbitrary")),
        name="gdn_mixer",
    )(qkv, z, ab, conv_w, a_log_pad, dt_bias_pad, o_gain)


OUT_TM = 512


def _out_kernel(og_ref, x_ref, gate_ref, g_ref, w_ref, o_ref):
    y = jnp.dot(og_ref[0], w_ref[...], preferred_element_type=F32)
    yn = y * lax.rsqrt(jnp.mean(y * y, axis=-1, keepdims=True) + NORM_EPS) * g_ref[...]
    o_ref[0] = x_ref[0] + gate_ref[0] * yn


def _out_proj(og, x, gate, gain, w):
    bsz, seq, d = x.shape
    tm = OUT_TM
    row = lambda b, i: (b, i, 0)
    vec = lambda b, i: (b, 0, 0)
    const = lambda b, i: (0, 0)
    return pl.pallas_call(
        _out_kernel,
        grid=(bsz, seq // tm),
        in_specs=[pl.BlockSpec((1, tm, og.shape[-1]), row),
                  pl.BlockSpec((1, tm, d), row),
                  pl.BlockSpec((1, 1, d), vec),
                  pl.BlockSpec((1, d), const),
                  pl.BlockSpec(w.shape, const)],
        out_specs=pl.BlockSpec((1, tm, d), row),
        out_shape=jax.ShapeDtypeStruct((bsz, seq, d), F32),
        compiler_params=_cparams(("parallel", "parallel")),
        name="out_proj",
    )(og, x, gate, gain, w)


KV_TM = 512
KV_HEAD_GROUP = 4
ATT_T = 512


def _fparts(f):
    hi = f.astype(BF16).astype(F32)
    r1 = f - hi
    mid = r1.astype(BF16).astype(F32)
    lo = (r1 - mid).astype(BF16).astype(F32)
    lane = lax.broadcasted_iota(jnp.int32, f.shape, 1)
    out = jnp.where(lane < B_HEADS, hi, 0.0)
    out = out + jnp.where(jnp.logical_and(lane >= B_HEADS, lane < 2 * B_HEADS),
                          pltpu.roll(mid, B_HEADS, 1), 0.0)
    out = out + jnp.where(jnp.logical_and(lane >= 2 * B_HEADS, lane < 3 * B_HEADS),
                          pltpu.roll(lo, 2 * B_HEADS, 1), 0.0)
    return out


def _kv_kernel(x_ref, shift_ref, scale_ref, g_ref, wk_ref, wvt_ref, wf_ref, bf_ref, place_ref,
               ones_ref, k_ref, vt_ref, fcat_ref, carry_ref):
    tm = x_ref.shape[1]

    @pl.when(pl.program_id(1) == 0)
    def _():
        carry_ref[...] = jnp.zeros_like(carry_ref)

    h = _norm_mod(x_ref[0], g_ref[...], scale_ref[0], shift_ref[0]).astype(BF16)
    for grp in range(B_HEADS // KV_HEAD_GROUP):
        rows = slice(grp * KV_HEAD_GROUP * V_ROWS, (grp + 1) * KV_HEAD_GROUP * V_ROWS)
        vt = lax.dot_general(wvt_ref[rows, :], h, (((1,), (1,)), ((), ())),
                             preferred_element_type=F32)
        r = lax.broadcasted_iota(jnp.int32, vt.shape, 0)
        for hh in range(KV_HEAD_GROUP):
            vt = jnp.where(r == hh * V_ROWS + B_DH, 1.0, vt)
        vt = vt.astype(BF16)
        for hh in range(KV_HEAD_GROUP):
            for jb in range(tm // ATT_T):
                vt_ref[0, grp * KV_HEAD_GROUP + hh, jb] = vt[hh * V_ROWS:(hh + 1) * V_ROWS,
                                                             jb * ATT_T:(jb + 1) * ATT_T]
    f =jnp.dot(h, wf_ref[...], preferred_element_type=F32) + bf_ref[...]
    lane = lax.broadcasted_iota(jnp.int32, f.shape, 1)
    log_f = jnp.where(lane < B_HEADS, -_softplus(-f), 0.0)
    row = lax.broadcasted_iota(jnp.int32, (tm, tm), 0)
    col = lax.broadcasted_iota(jnp.int32, (tm, tm), 1)
    tril = jnp.where(row >= col, 1.0, 0.0).astype(F32)
    f_cum = jnp.dot(tril, log_f, precision=lax.Precision.HIGHEST,
                    preferred_element_type=F32) + carry_ref[...]
    carry_ref[...] = f_cum[tm - 1:tm, :]
    fcat = _fparts(f_cum).astype(BF16)
    fcat_ref[0] = fcat
    for j in range(k_ref.shape[2] // 512):
        sl = slice(j * 512, (j + 1) * 512)
        k_aug = (jnp.dot(h, wk_ref[:, sl], preferred_element_type=F32)
                 + jnp.dot(fcat, place_ref[:, sl], preferred_element_type=F32) + ones_ref[:, sl])
        k_ref[0, :, sl] = k_aug.astype(BF16)


def _kv_proj(x, shift, scale, gain, wk_aug, wvt, wf, bf, place_k, ones_k):
    bsz, seq, d = x.shape
    tm = KV_TM
    row = lambda b, i: (b, i, 0)
    vec = lambda b, i: (b, 0, 0)
    const = lambda b, i: (0, 0)
    kw = B_HEADS * AUG_W
    nblk = tm // ATT_T
    return pl.pallas_call(
        _kv_kernel,
        grid=(bsz, seq // tm),
        in_specs=[pl.BlockSpec((1, tm, d), row),
                  pl.BlockSpec((1, 1, d), vec),
                  pl.BlockSpec((1, 1, d), vec),
                  pl.BlockSpec((1, d), const),
                  pl.BlockSpec(wk_aug.shape, const),
                  pl.BlockSpec(wvt.shape, const),
                  pl.BlockSpec(wf.shape, const),
                  pl.BlockSpec((1, LANES), const),
                  pl.BlockSpec(place_k.shape, const),
                  pl.BlockSpec((1, kw), const)],
        out_specs=[pl.BlockSpec((1, tm, kw), row),
                   pl.BlockSpec((1, B_HEADS, nblk, V_ROWS, ATT_T), lambda b, i: (b, 0, i, 0, 0)),
                   pl.BlockSpec((1, tm, LANES), row)],
        out_shape=[jax.ShapeDtypeStruct((bsz, seq, kw), BF16),
                   jax.ShapeDtypeStruct((bsz, B_HEADS, seq // ATT_T, V_ROWS, ATT_T), BF16),
                   jax.ShapeDtypeStruct((bsz, seq, LANES), BF16)],
        scratch_shapes=[pltpu.VMEM((1, LANES), F32)],
        compiler_params=_cparams(("parallel", "arbitrary")),
        name="kv_proj",
    )(x, shift, scale, gain, wk_aug, wvt, wf, bf, place_k, ones_k)


B_IN_TM = 512


def _b_in_kernel(x_ref, shift_ref, scale_ref, g_ref, wq_ref, wz_ref, fcat_ref, place_ref, ones_ref,
                 q_ref, z_ref):
    h = _norm_mod(x_ref[0], g_ref[...], scale_ref[0], shift_ref[0]).astype(BF16)
    fcat = fcat_ref[0]
    for j in range(q_ref.shape[2] // 512):
        sl = slice(j * 512, (j + 1) * 512)
        q_aug = (jnp.dot(h, wq_ref[:, sl], preferred_element_type=F32)
                 + jnp.dot(fcat, place_ref[:, sl], preferred_element_type=F32) + ones_ref[:, sl])
        q_ref[0, :, sl] = q_aug.astype(BF16)
    for j in range(z_ref.shape[2] // 512):
        sl = slice(j * 512, (j + 1) * 512)
        z_ref[0, :, sl] = jnp.dot(h, wz_ref[:, sl], preferred_element_type=F32).astype(BF16)


def _b_in_proj(x, shift, scale, gain, wq_aug, wz, fcat, place_q, ones_q):
    bsz, seq, d = x.shape
    tm = B_IN_TM
    row = lambda b, i: (b, i, 0)
    vec = lambda b, i: (b, 0, 0)
    const = lambda b, i: (0, 0)
    qw = B_HEADS * AUG_W
    return pl.pallas_call(
        _b_in_kernel,
        grid=(bsz, seq // tm),
        in_specs=[pl.BlockSpec((1, tm, d), row),
                  pl.BlockSpec((1, 1, d), vec),
                  pl.BlockSpec((1, 1, d), vec),
                  pl.BlockSpec((1, d), const),
                  pl.BlockSpec(wq_aug.shape, const),
                  pl.BlockSpec(wz.shape, const),
                  pl.BlockSpec((1, tm, LANES), row),
                  pl.BlockSpec(place_q.shape, const),
                  pl.BlockSpec((1, qw), const)],
        out_specs=[pl.BlockSpec((1, tm, qw), row),
                   pl.BlockSpec((1, tm, B_W), row)],
        out_shape=[jax.ShapeDtypeStruct((bsz, seq, qw), BF16),
                   jax.ShapeDtypeStruct((bsz, seq, B_W), BF16)],
        compiler_params=_cparams(("parallel", "parallel")),
        name="b_in_proj",
    )(x, shift, scale, gain, wq_aug, wz, fcat, place_q, ones_q)


HEADS_PER_STEP = 4


def _attn_kernel(q_ref, k_ref, vt_ref, z_ref, o_ref, s_ref, p_ref, acc_ref):
    t = ATT_T
    i = pl.program_id(2)
    qs = [q_ref[0, :, hh * AUG_W:(hh + 1) * AUG_W] for hh in range(HEADS_PER_STEP)]

    hds = range(HEADS_PER_STEP)

    def scores(jb):
        start = pl.multiple_of(jb * t, t)
        return [lax.dot_general(k_ref[0, pl.ds(start, t), hh * AUG_W:(hh + 1) * AUG_W], qs[hh],
                                (((1,), (1,)), ((), ())), preferred_element_type=F32) for hh in hds]

    def weighted_values(jb, slot):
        return [jnp.dot(vt_ref[0, hh, jb], p_ref[slot, hh], preferred_element_type=F32)
                for hh in hds]

    def softmax_update(ms, masked, slot):
        ss = [s_ref[hh] for hh in hds]
        if masked:
            key = lax.broadcasted_iota(jnp.int32, (t, t), 0)
            qry = lax.broadcasted_iota(jnp.int32, (t, t), 1)
            ss = [jnp.where(key <= qry, s, -jnp.inf) for s in ss]
        m_news = [jnp.maximum(ms[hh], jnp.max(ss[hh], axis=0, keepdims=True)) for hh in hds]
        for hh in hds:
            p_ref[slot, hh] = jnp.exp(ss[hh] - m_news[hh]).astype(BF16)
        return m_news, [jnp.exp(ms[hh] - m_news[hh]) for hh in hds]

    def accumulate(alphas, pvs):
        for hh in hds:
            acc_ref[hh] = alphas[hh] * acc_ref[hh] + pvs[hh]

    def step(j, slot, carry):
        ms, alphas = carry
        pvs = weighted_values(jnp.maximum(j - 1, 0), 1 - slot)
        ss_next = scores(j + 1)
        ms, alphas_new = softmax_update(ms, False, slot)
        accumulate(alphas, pvs)
        for hh in hds:
            s_ref[hh] = ss_next[hh]
        return ms, alphas_new

    def diagonal(slot, carry):
        ms, alphas = carry
        pvs = weighted_values(jnp.maximum(i - 1, 0), 1 - slot)
        _, alphas_new = softmax_update(ms, True, slot)
        accumulate(alphas, pvs)
        accumulate(alphas_new, weighted_values(i, slot))

    ss0 = scores(0)
    for hh in hds:
        s_ref[hh] = ss0[hh]
        p_ref[1, hh] = jnp.zeros((t, t), BF16)
        acc_ref[hh] = jnp.zeros((V_ROWS, t), F32)
    init = ([jnp.full((1, t), -jnp.inf, F32) for _ in hds], [jnp.ones((1, t), F32) for _ in hds])
    carry = lax.fori_loop(0, i // 2, lambda jj, cr: step(2 * jj + 1, 1, step(2 * jj, 0, cr)), init)

    @pl.when(i % 2 == 0)
    def _():
        diagonal(0, carry)

    @pl.when(i % 2 == 1)
    def _():
        diagonal(1, step(i - 1, 0, carry))

    o_t = jnp.concatenate([acc_ref[hh][:B_DH, :] / acc_ref[hh][B_DH:B_DH + 1, :] for hh in hds],
                          axis=0)
    o_ref[0] = (o_t.T * _silu(z_ref[0].astype(F32))).astype(BF16)


def _fox_attention(q_aug, k_aug, vt, z):
    bsz, seq, _ = z.shape
    t = ATT_T
    n_pairs = B_HEADS // HEADS_PER_STEP
    pw = HEADS_PER_STEP * AUG_W
    return pl.pallas_call(
        _attn_kernel,
        grid=(bsz, n_pairs, seq // t),
        in_specs=[pl.BlockSpec((1, t, pw), lambda b, p, i: (b, i, p)),
                  pl.BlockSpec((1, seq, pw), lambda b, p, i: (b, 0, p)),
                  pl.BlockSpec((1, HEADS_PER_STEP, seq // t, V_ROWS, t),
                               lambda b, p, i: (b, p, 0, 0, 0)),
                  pl.BlockSpec((1, t, HEADS_PER_STEP * B_DH), lambda b, p, i: (b, i, p))],
        out_specs=pl.BlockSpec((1, t, HEADS_PER_STEP * B_DH), lambda b, p, i: (b, i, p)),
        out_shape=jax.ShapeDtypeStruct((bsz, seq, B_W), BF16),
        scratch_shapes=[pltpu.VMEM((HEADS_PER_STEP, t, t), F32),
                        pltpu.VMEM((2, HEADS_PER_STEP, t, t), BF16),
                        pltpu.VMEM((HEADS_PER_STEP, V_ROWS, t), F32)],
        compiler_params=_cparams(("parallel", "parallel", "arbitrary")),
        name="fox_attention",
    )(q_aug, k_aug, vt, z)


def _pad_cols(w, n):
    return jnp.pad(w, ((0, 0), (0, n - w.shape[1])))


def _aug_weight(w, scale, width=AUG_W):
    d = w.shape[0]
    w = (w * scale).reshape(d, B_HEADS, B_DH)
    w = jnp.pad(w, ((0, 0), (0, 0), (0, width - B_DH)))
    return w.reshape(d, B_HEADS * width).astype(BF16)


def _placement(f_off, one_off, sign):
    place = np.zeros((LANES, B_HEADS * AUG_W), np.float32)
    ones = np.zeros((1, B_HEADS * AUG_W), np.float32)
    for p in range(N_FPARTS):
        for h in range(B_HEADS):
            place[p * B_HEADS + h, h * AUG_W + f_off + p] = sign
            ones[0, h * AUG_W + one_off + p] = 1.0
    return jnp.asarray(place, BF16), jnp.asarray(ones, F32)


def _vec3(m, n):
    return [t[:, None, :] for t in jnp.split(m, n, axis=-1)]


def kernel(x, c, a_mod_w, a_mod_b, a_pre_g, a_post_g, a_w_in, a_conv_w, a_a_log, a_dt_bias, a_o_gain, a_w_out, kv_mod_w, kv_mod_b, kv_norm_g, kv_w, kv_b_f, b_mod_w, b_mod_b, b_pre_g, b_post_g, b_w_in, b_w_out):
    place_q, ones_q = _placement(AUG_F0, AUG_F0 + N_FPARTS, 1.0)
    place_k, ones_k = _placement(AUG_F0 + N_FPARTS, AUG_F0, -1.0)

    for j in range(N_A_LAYERS):
        shift, scale, gate = _vec3(_adaln(c, a_mod_w[j], a_mod_b[j]), 3)
        w_in = a_w_in[j]
        n_main = A_CONV_CH + A_V
        w_main = w_in[:, :n_main].astype(BF16)
        w_ab = _pad_cols(w_in[:, n_main:], LANES).astype(BF16)
        qkv, z, ab = _a_in_proj(x, shift, scale, a_pre_g[j][None, :], w_main, w_ab)
        a_log_pad = _pad_cols(a_a_log[j][None, :], LANES)
        dt_pad = _pad_cols(a_dt_bias[j][None, :], LANES)
        og = _gdn_mixer(qkv, z, ab, a_conv_w[j], a_log_pad, dt_pad, a_o_gain[j][None, :])
        x = _out_proj(og, x, gate, a_post_g[j][None, :], a_w_out[j].astype(BF16))

    shift, scale = _vec3(_adaln(c, kv_mod_w, kv_mod_b), 2)
    wk_aug = _aug_weight(kv_w[:, :B_W], 1.0)
    wvt = _aug_weight(kv_w[:, B_W:2 * B_W], 1.0, V_ROWS).T
    wf = _pad_cols(kv_w[:, 2 * B_W:], LANES).astype(BF16)
    bf = _pad_cols(kv_b_f[None, :], LANES)
    k_aug, vt, fcat = _kv_proj(x, shift, scale, kv_norm_g[None, :], wk_aug, wvt, wf, bf, place_k, ones_k)

    for j in range(N_B_LAYERS):
        shift, scale, gate = _vec3(_adaln(c, b_mod_w[j], b_mod_b[j]), 3)
        wq_aug = _aug_weight(b_w_in[j][:, :B_W], B_DH ** -0.5)
        wz = b_w_in[j][:, B_W:].astype(BF16)
        q_aug, z = _b_in_proj(x, shift, scale, b_pre_g[j][None, :], wq_aug, wz, fcat, place_q, ones_q)
        og = _fox_attention(q_aug, k_aug, vt, z)
        x = _out_proj(og, x, gate, b_post_g[j][None, :], b_w_out[j].astype(BF16))
    return x
```

```python
import functools

import jax
import jax.numpy as jnp
import numpy as np
from jax import lax
from jax.experimental import pallas as pl
from jax.experimental.pallas import tpu as pltpu

F32 = jnp.float32
BF16 = jnp.bfloat16

D_MODEL = 1024
N_A_LAYERS = 2
N_B_LAYERS = 2
A_HEADS = 8
A_DK = 128
A_DV = 128
A_QK = A_HEADS * A_DK
A_V = A_HEADS * A_DV
A_CONV = 4
A_CONV_CH = 2 * A_QK + A_V
B_HEADS = 16
B_DH = 64
B_W = B_HEADS * B_DH
NORM_EPS = 1e-6

LANES = 128
SUBLANES = 8
GDN_CHUNK = 128
GDN_TILE = 512
GDN_GROUP = 2
VMEM_LIMIT = 56 * 1024 * 1024

AUG_W = 128
V_ROWS = 80
AUG_F0 = B_DH
N_FPARTS = 3


def _cparams(sem):
    return pltpu.CompilerParams(dimension_semantics=sem, vmem_limit_bytes=VMEM_LIMIT)


def _silu(x):
    return x / (1.0 + jnp.exp(-x))


def _softplus(x):
    return jnp.maximum(x, 0.0) + jnp.log(1.0 + jnp.exp(-jnp.abs(x)))


def _dot(a, b):
    return jnp.dot(a.astype(BF16), b.astype(BF16), preferred_element_type=F32)


def _dot_nt(a, b):
    return lax.dot_general(a.astype(BF16), b.astype(BF16), (((1,), (1,)), ((), ())),
                           preferred_element_type=F32)


def _dot_tn(a, b):
    return lax.dot_general(a.astype(BF16), b.astype(BF16), (((0,), (0,)), ((), ())),
                           preferred_element_type=F32)


def _split2(a):
    hi = a.astype(BF16)
    lo = (a - hi.astype(F32)).astype(BF16)
    return hi, lo


def _dot3(a, b):
    ah, al = _split2(a)
    bh, bl = _split2(b)
    mm = functools.partial(jnp.dot, preferred_element_type=F32)
    return mm(ah, bh) + (mm(ah, bl) + mm(al, bh))


def _norm_mod(x, gain, scale, shift):
    ms = jnp.mean(x * x, axis=-1, keepdims=True)
    y = x * lax.rsqrt(ms + NORM_EPS) * gain
    return y * (1.0 + scale) + shift


def _mod_kernel(c_ref, w_ref, b_ref, o_ref):
    s = _silu(c_ref[...])
    o_ref[...] = jnp.dot(s, w_ref[...], precision=lax.Precision.HIGHEST,
                         preferred_element_type=F32) + b_ref[...]


def _adaln(c, w, b):
    bsz, d = c.shape
    n = w.shape[1]
    tn = 512
    return pl.pallas_call(
        _mod_kernel,
        grid=(n // tn,),
        in_specs=[pl.BlockSpec((bsz, d), lambda j: (0, 0)),
                  pl.BlockSpec((d, tn), lambda j: (0, j)),
                  pl.BlockSpec((1, tn), lambda j: (0, j))],
        out_specs=pl.BlockSpec((bsz, tn), lambda j: (0, j)),
        out_shape=jax.ShapeDtypeStruct((bsz, n), F32),
        compiler_params=_cparams(("arbitrary",)),
        name="adaln_mod",
    )(c, w, b.reshape(1, n))


A_IN_TM = 512
A_IN_TN = 512


def _a_in_kernel(x_ref, shift_ref, scale_ref, g_ref, w_ref, wab_ref, qkv_ref, z_ref, ab_ref):
    h = _norm_mod(x_ref[0], g_ref[...], scale_ref[0], shift_ref[0]).astype(BF16)
    for j in range(A_CONV_CH // A_IN_TN):
        sl = slice(j * A_IN_TN, (j + 1) * A_IN_TN)
        qkv_ref[0, :, sl] = jnp.dot(h, w_ref[:, sl], preferred_element_type=F32).astype(BF16)
    for j in range(A_V // A_IN_TN):
        sl = slice(j * A_IN_TN, (j + 1) * A_IN_TN)
        wsl = slice(A_CONV_CH + j * A_IN_TN, A_CONV_CH + (j + 1) * A_IN_TN)
        z_ref[0, :, sl] = jnp.dot(h, w_ref[:, wsl], preferred_element_type=F32).astype(BF16)
    ab_ref[0] = jnp.dot(h, wab_ref[...], preferred_element_type=F32)


def _a_in_proj(x, shift, scale, gain, w_main, w_ab):
    bsz, seq, d = x.shape
    tm = A_IN_TM
    row = lambda b, i: (b, i, 0)
    vec = lambda b, i: (b, 0, 0)
    const = lambda b, i: (0, 0)
    return pl.pallas_call(
        _a_in_kernel,
        grid=(bsz, seq // tm),
        in_specs=[pl.BlockSpec((1, tm, d), row),
                  pl.BlockSpec((1, 1, d), vec),
                  pl.BlockSpec((1, 1, d), vec),
                  pl.BlockSpec((1, d), const),
                  pl.BlockSpec(w_main.shape, const),
                  pl.BlockSpec(w_ab.shape, const)],
        out_specs=[pl.BlockSpec((1, tm, A_CONV_CH), row),
                   pl.BlockSpec((1, tm, A_V), row),
                   pl.BlockSpec((1, tm, LANES), row)],
        out_shape=[jax.ShapeDtypeStruct((bsz, seq, A_CONV_CH), BF16),
                   jax.ShapeDtypeStruct((bsz, seq, A_V), BF16),
                   jax.ShapeDtypeStruct((bsz, seq, LANES), F32)],
        compiler_params=_cparams(("parallel", "parallel")),
        name="a_in_proj",
    )(x, shift, scale, gain, w_main, w_ab)


def _inv_unit_lower(a_list, row, col):
    c = a_list[0].shape[0]
    eye = jnp.where(row == col, 1.0, 0.0).astype(F32)

    def blk(shift):
        return (row >> shift) == (col >> shift)

    b_prev = blk(3)
    ads = [jnp.where(b_prev, a, 0.0) for a in a_list]
    adb = [ad.astype(BF16) for ad in ads]
    a2s = [_dot(x, x) for x in adb]
    a2b = [a2.astype(BF16) for a2 in a2s]
    a3s = [_dot(x, y) for x, y in zip(adb, a2b)]
    a4s = [_dot(y, y) for y in a2b]
    p1s = [eye - ad + a2 - a3 for ad, a2, a3 in zip(ads, a2s, a3s)]
    ts = [p1 + _dot(p1, a4) for p1, a4 in zip(p1s, a4s)]
    shift = 4
    while (1 << shift) <= c:
        b_cur = blk(shift)
        sel = jnp.logical_and(b_cur, jnp.logical_not(b_prev))
        offs = [jnp.where(sel, a, 0.0).astype(BF16) for a in a_list]
        tbs = [t.astype(BF16) for t in ts]
        xs = [_dot(off, tb) for off, tb in zip(offs, tbs)]
        ts = [t - _dot(tb, x) for t, tb, x in zip(ts, tbs, xs)]
        b_prev = b_cur
        shift += 1
    rs = [eye - t - _dot3(a, t) for a, t in zip(a_list, ts)]
    return [t + _dot(t, r) for t, r in zip(ts, rs)]


def _gdn_kernel(qkv_ref, z_ref, ab_ref, convw_ref, alog_ref, dtb_ref, ogain_ref, o_ref,
                xe_ref, state_ref):
    c = GDN_CHUNK
    tile = GDN_TILE
    t_idx = pl.program_id(1)

    @pl.when(t_idx == 0)
    def _():
        xe_ref[0:SUBLANES, :] = jnp.zeros((SUBLANES, A_CONV_CH), F32)
        state_ref[...] = jnp.zeros_like(state_ref)

    x = qkv_ref[0].astype(F32)
    xe_ref[SUBLANES:SUBLANES + tile, :] = x
    y = convw_ref[A_CONV - 1:A_CONV, :] * x
    for k in range(A_CONV - 1):
        off = SUBLANES - (A_CONV - 1) + k
        y = y + convw_ref[k:k + 1, :] * xe_ref[off:off + tile, :]
    xe_ref[0:SUBLANES, :] = x[tile - SUBLANES:tile, :]
    y = _silu(y)

    ab = ab_ref[0]
    g = -jnp.exp(alog_ref[...]) * _softplus(ab + dtb_ref[...])
    beta = 1.0 / (1.0 + jnp.exp(-ab))
    row = lax.broadcasted_iota(jnp.int32, (c, c), 0)
    col = lax.broadcasted_iota(jnp.int32, (c, c), 1)
    tril = jnp.where(row >= col, 1.0, 0.0).astype(F32)
    lower = row >= col
    strict = row > col

    chunks = range(tile // c)
    heads = range(A_HEADS)
    probs = [(ci, h) for ci in chunks for h in heads]
    gcs, egs, e_lasts, e_rests = [], [], [], []
    for ci in chunks:
        gc = jnp.dot(tril, g[ci * c:(ci + 1) * c], precision=lax.Precision.HIGHEST,
                     preferred_element_type=F32)
        g_last = gc[c - 1:c, :]
        gcs.append((gc, gc.T))
        egs.append(jnp.exp(gc))
        e_lasts.append(jnp.exp(g_last))
        e_rests.append(jnp.exp(g_last - gc))
    qs, ks, qks, sols = [], [], [], []
    for g0 in range(0, len(probs), GDN_GROUP * A_HEADS):
        grp = probs[g0:g0 + GDN_GROUP * A_HEADS]
        gq, gk, kbs, rhss, decs = [], [], [], [], []
        for ci, h in grp:
            rows = slice(ci * c, (ci + 1) * c)
            q = y[rows, h * A_DK:(h + 1) * A_DK]
            k = y[rows, A_QK + h * A_DK:A_QK + (h + 1) * A_DK]
            v = y[rows, 2 * A_QK + h * A_DV:2 * A_QK + (h + 1) * A_DV]
            q = q * (lax.rsqrt(jnp.sum(q * q, axis=-1, keepdims=True) + NORM_EPS) * (A_DK ** -0.5))
            k = k * lax.rsqrt(jnp.sum(k * k, axis=-1, keepdims=True) + NORM_EPS)
            b_col = beta[rows, A_HEADS + h:A_HEADS + h + 1]
            kb = k * b_col
            gc, gc_t = gcs[ci]
            diff = gc[:, h:h + 1] - gc_t[h:h + 1, :]
            decs.append(jnp.exp(jnp.where(lower, diff, -jnp.inf)))
            gq.append(q)
            gk.append(k)
            kbs.append(kb)
            rhss.append(jnp.concatenate([v * b_col, kb * egs[ci][:, h:h + 1]], axis=1))
        ms = [_dot_nt(jnp.concatenate([kb, q], axis=0), k) for kb, q, k in zip(kbs, gq, gk)]
        a_list = [jnp.where(strict, m[:c] * dec, 0.0) for m, dec in zip(ms, decs)]
        qks += [m[c:] * dec for m, dec in zip(ms, decs)]
        t_invs = _inv_unit_lower(a_list, row, col)
        sols += [_dot(t, rhs) for t, rhs in zip(t_invs, rhss)]
        qs += gq
        ks += gk
    states = [state_ref[h] for h in heads]
    outs = []
    for ci in chunks:
        pr = [ci * A_HEADS + h for h in heads]
        wqs = [_dot(jnp.concatenate([sols[p][:, A_DV:], qs[p] * egs[ci][:, h:h + 1]], axis=0),
                    states[h]) for h, p in zip(heads, pr)]
        v_news = [sols[p][:, :A_DV] - wq[:c] for p, wq in zip(pr, wqs)]
        outs += [wq[c:] + _dot(qks[p], vn) for p, wq, vn in zip(pr, wqs, v_news)]
        upds = [_dot_tn(ks[p] * e_rests[ci][:, h:h + 1], vn) for h, p, vn in zip(heads, pr, v_news)]
        states = [states[h] * e_lasts[ci][:, h:h + 1] + upds[h] for h in heads]
    for h in heads:
        state_ref[h] = states[h]
    for (ci, h), o in zip(probs, outs):
        rows = slice(ci * c, (ci + 1) * c)
        sl = slice(h * A_DV, (h + 1) * A_DV)
        o = o * lax.rsqrt(jnp.mean(o * o, axis=-1, keepdims=True) + NORM_EPS) * ogain_ref[...]
        zz = z_ref[0, rows, sl].astype(F32)
        o_ref[0, rows, sl] = (o * _silu(zz)).astype(BF16)


def _gdn_mixer(qkv, z, ab, conv_w, a_log_pad, dt_bias_pad, o_gain):
    bsz, seq, _ = qkv.shape
    c = GDN_TILE
    row = lambda b, i: (b, i, 0)
    const = lambda b, i: (0, 0)
    return pl.pallas_call(
        _gdn_kernel,
        grid=(bsz, seq // c),
        in_specs=[pl.BlockSpec((1, c, A_CONV_CH), row),
                  pl.BlockSpec((1, c, A_V), row),
                  pl.BlockSpec((1, c, LANES), row),
                  pl.BlockSpec((A_CONV, A_CONV_CH), const),
                  pl.BlockSpec((1, LANES), const),
                  pl.BlockSpec((1, LANES), const),
                  pl.BlockSpec((1, A_DV), const)],
        out_specs=pl.BlockSpec((1, c, A_V), row),
        out_shape=jax.ShapeDtypeStruct((bsz, seq, A_V), BF16),
        scratch_shapes=[pltpu.VMEM((SUBLANES + c, A_CONV_CH), F32),
                        pltpu.VMEM((A_HEADS, A_DK, A_DV), F32)],
        compiler_params=_cparams(("parallel", "arbitrary")),
        name="gdn_mixer",
    )(qkv, z, ab, conv_w, a_log_pad, dt_bias_pad, o_gain)


OUT_TM = 512


def _out_kernel(og_ref, x_ref, gate_ref, g_ref, w_ref, o_ref):
    y = jnp.dot(og_ref[0], w_ref[...], preferred_element_type=F32)
    yn = y * lax.rsqrt(jnp.mean(y * y, axis=-1, keepdims=True) + NORM_EPS) * g_ref[...]
    o_ref[0] = x_ref[0] + gate_ref[0] * yn


def _out_proj(og, x, gate, gain, w):
    bsz, seq, d = x.shape
    tm = OUT_TM
    row = lambda b, i: (b, i, 0)
    vec = lambda b, i: (b, 0, 0)
    const = lambda b, i: (0, 0)
    return pl.pallas_call(
        _out_kernel,
        grid=(bsz, seq // tm),
        in_specs=[pl.BlockSpec((1, tm, og.shape[-1]), row),
                  pl.BlockSpec((1, tm, d), row),
                  pl.BlockSpec((1, 1, d), vec),
                  pl.BlockSpec((1, d), const),
                  pl.BlockSpec(w.shape, const)],
        out_specs=pl.BlockSpec((1, tm, d), row),
        out_shape=jax.ShapeDtypeStruct((bsz, seq, d), F32),
        compiler_params=_cparams(("parallel", "parallel")),
        name="out_proj",
    )(og, x, gate, gain, w)


KV_TM = 512
KV_HEAD_GROUP = 4
ATT_T = 512


def _fparts(f):
    hi = f.astype(BF16).astype(F32)
    r1 = f - hi
    mid = r1.astype(BF16).astype(F32)
    lo = (r1 - mid).astype(BF16).astype(F32)
    lane = lax.broadcasted_iota(jnp.int32, f.shape, 1)
    out = jnp.where(lane < B_HEADS, hi, 0.0)
    out = out + jnp.where(jnp.logical_and(lane >= B_HEADS, lane < 2 * B_HEADS),
                          pltpu.roll(mid, B_HEADS, 1), 0.0)
    out = out + jnp.where(jnp.logical_and(lane >= 2 * B_HEADS, lane < 3 * B_HEADS),
                          pltpu.roll(lo, 2 * B_HEADS, 1), 0.0)
    return out


def _kv_kernel(x_ref, shift_ref, scale_ref, g_ref, wk_ref, wvt_ref, wf_ref, bf_ref, place_ref,
               ones_ref, k_ref, vt_ref, fcat_ref, carry_ref):
    tm = x_ref.shape[1]

    @pl.when(pl.program_id(1) == 0)
    def _():
        carry_ref[...] = jnp.zeros_like(carry_ref)

    h = _norm_mod(x_ref[0], g_ref[...], scale_ref[0], shift_ref[0]).astype(BF16)
    for grp in range(B_HEADS // KV_HEAD_GROUP):
        rows = slice(grp * KV_HEAD_GROUP * V_ROWS, (grp + 1) * KV_HEAD_GROUP * V_ROWS)
        vt = lax.dot_general(wvt_ref[rows, :], h, (((1,), (1,)), ((), ())),
                             preferred_element_type=F32)
        r = lax.broadcasted_iota(jnp.int32, vt.shape, 0)
        for hh in range(KV_HEAD_GROUP):
            vt = jnp.where(r == hh * V_ROWS + B_DH, 1.0, vt)
        vt = vt.astype(BF16)
        for hh in range(KV_HEAD_GROUP):
            for jb in range(tm // ATT_T):
                vt_ref[0, grp * KV_HEAD_GROUP + hh, jb] = vt[hh * V_ROWS:(hh + 1) * V_ROWS,
                                                             jb * ATT_T:(jb + 1) * ATT_T]
    f =jnp.dot(h, wf_ref[...], preferred_element_type=F32) + bf_ref[...]
    lane = lax.broadcasted_iota(jnp.int32, f.shape, 1)
    log_f = jnp.where(lane < B_HEADS, -_softplus(-f), 0.0)
    row = lax.broadcasted_iota(jnp.int32, (tm, tm), 0)
    col = lax.broadcasted_iota(jnp.int32, (tm, tm), 1)
    tril = jnp.where(row >= col, 1.0, 0.0).astype(F32)
    f_cum = jnp.dot(tril, log_f, precision=lax.Precision.HIGHEST,
                    preferred_element_type=F32) + carry_ref[...]
    carry_ref[...] = f_cum[tm - 1:tm, :]
    fcat = _fparts(f_cum).astype(BF16)
    fcat_ref[0] = fcat
    for j in range(k_ref.shape[2] // 512):
        sl = slice(j * 512, (j + 1) * 512)
        k_aug = (jnp.dot(h, wk_ref[:, sl], preferred_element_type=F32)
                 + jnp.dot(fcat, place_ref[:, sl], preferred_element_type=F32) + ones_ref[:, sl])
        k_ref[0, :, sl] = k_aug.astype(BF16)


def _kv_proj(x, shift, scale, gain, wk_aug, wvt, wf, bf, place_k, ones_k):
    bsz, seq, d = x.shape
    tm = KV_TM
    row = lambda b, i: (b, i, 0)
    vec = lambda b, i: (b, 0, 0)
    const = lambda b, i: (0, 0)
    kw = B_HEADS * AUG_W
    nblk = tm // ATT_T
    return pl.pallas_call(
        _kv_kernel,
        grid=(bsz, seq // tm),
        in_specs=[pl.BlockSpec((1, tm, d), row),
                  pl.BlockSpec((1, 1, d), vec),
                  pl.BlockSpec((1, 1, d), vec),
                  pl.BlockSpec((1, d), const),
                  pl.BlockSpec(wk_aug.shape, const),
                  pl.BlockSpec(wvt.shape, const),
                  pl.BlockSpec(wf.shape, const),
                  pl.BlockSpec((1, LANES), const),
                  pl.BlockSpec(place_k.shape, const),
                  pl.BlockSpec((1, kw), const)],
        out_specs=[pl.BlockSpec((1, tm, kw), row),
                   pl.BlockSpec((1, B_HEADS, nblk, V_ROWS, ATT_T), lambda b, i: (b, 0, i, 0, 0)),
                   pl.BlockSpec((1, tm, LANES), row)],
        out_shape=[jax.ShapeDtypeStruct((bsz, seq, kw), BF16),
                   jax.ShapeDtypeStruct((bsz, B_HEADS, seq // ATT_T, V_ROWS, ATT_T), BF16),
                   jax.ShapeDtypeStruct((bsz, seq, LANES), BF16)],
        scratch_shapes=[pltpu.VMEM((1, LANES), F32)],
        compiler_params=_cparams(("parallel", "arbitrary")),
        name="kv_proj",
    )(x, shift, scale, gain, wk_aug, wvt, wf, bf, place_k, ones_k)


B_IN_TM = 512


def _b_in_kernel(x_ref, shift_ref, scale_ref, g_ref, wq_ref, wz_ref, fcat_ref, place_ref, ones_ref,
                 q_ref, z_ref):
    h = _norm_mod(x_ref[0], g_ref[...], scale_ref[0], shift_ref[0]).astype(BF16)
    fcat = fcat_ref[0]
    for j in range(q_ref.shape[2] // 512):
        sl = slice(j * 512, (j + 1) * 512)
        q_aug = (jnp.dot(h, wq_ref[:, sl], preferred_element_type=F32)
                 + jnp.dot(fcat, place_ref[:, sl], preferred_element_type=F32) + ones_ref[:, sl])
        q_ref[0, :, sl] = q_aug.astype(BF16)
    for j in range(z_ref.shape[2] // 512):
        sl = slice(j * 512, (j + 1) * 512)
        z_ref[0, :, sl] = jnp.dot(h, wz_ref[:, sl], preferred_element_type=F32).astype(BF16)


def _b_in_proj(x, shift, scale, gain, wq_aug, wz, fcat, place_q, ones_q):
    bsz, seq, d = x.shape
    tm = B_IN_TM
    row = lambda b, i: (b, i, 0)
    vec = lambda b, i: (b, 0, 0)
    const = lambda b, i: (0, 0)
    qw = B_HEADS * AUG_W
    return pl.pallas_call(
        _b_in_kernel,
        grid=(bsz, seq // tm),
        in_specs=[pl.BlockSpec((1, tm, d), row),
                  pl.BlockSpec((1, 1, d), vec),
                  pl.BlockSpec((1, 1, d), vec),
                  pl.BlockSpec((1, d), const),
                  pl.BlockSpec(wq_aug.shape, const),
                  pl.BlockSpec(wz.shape, const),
                  pl.BlockSpec((1, tm, LANES), row),
                  pl.BlockSpec(place_q.shape, const),
                  pl.BlockSpec((1, qw), const)],
        out_specs=[pl.BlockSpec((1, tm, qw), row),
                   pl.BlockSpec((1, tm, B_W), row)],
        out_shape=[jax.ShapeDtypeStruct((bsz, seq, qw), BF16),
                   jax.ShapeDtypeStruct((bsz, seq, B_W), BF16)],
        compiler_params=_cparams(("parallel", "parallel")),
        name="b_in_proj",
    )(x, shift, scale, gain, wq_aug, wz, fcat, place_q, ones_q)


HEADS_PER_STEP = 4


def _attn_kernel(q_ref, k_ref, vt_ref, z_ref, o_ref, s_ref, p_ref, acc_ref):
    t = ATT_T
    i = pl.program_id(2)
    qs = [q_ref[0, :, hh * AUG_W:(hh + 1) * AUG_W] for hh in range(HEADS_PER_STEP)]

    hds = range(HEADS_PER_STEP)

    def scores(jb):
        start = pl.multiple_of(jb * t, t)
        return [lax.dot_general(k_ref[0, pl.ds(start, t), hh * AUG_W:(hh + 1) * AUG_W], qs[hh],
                                (((1,), (1,)), ((), ())), preferred_element_type=F32) for hh in hds]

    def weighted_values(jb, slot):
        return [jnp.dot(vt_ref[0, hh, jb], p_ref[slot, hh], preferred_element_type=F32)
                for hh in hds]

    def block_max(ss):
        return [jnp.max(s, axis=0, keepdims=True) for s in ss]

    def softmax_update(ms, masked, slot, bms=None):
        ss = [s_ref[hh] for hh in hds]
        if masked:
            key = lax.broadcasted_iota(jnp.int32, (t, t), 0)
            qry = lax.broadcasted_iota(jnp.int32, (t, t), 1)
            ss = [jnp.where(key <= qry, s, -jnp.inf) for s in ss]
            bms = block_max(ss)
        m_news = [jnp.maximum(ms[hh], bms[hh]) for hh in hds]
        for hh in hds:
            p_ref[slot, hh] = jnp.exp(ss[hh] - m_news[hh]).astype(BF16)
        return m_news, [jnp.exp(ms[hh] - m_news[hh]) for hh in hds]

    def accumulate(alphas, pvs):
        for hh in hds:
            acc_ref[hh] = alphas[hh] * acc_ref[hh] + pvs[hh]

    def step(j, slot, carry):
        ms, alphas, bms = carry
        pvs = weighted_values(jnp.maximum(j - 1, 0), 1 - slot)
        ss_next = scores(j + 1)
        ms, alphas_new = softmax_update(ms, False, slot, bms)
        accumulate(alphas, pvs)
        for hh in hds:
            s_ref[hh] = ss_next[hh]
        return ms, alphas_new, block_max(ss_next)

    def diagonal(slot, carry):
        ms, alphas, _ = carry
        pvs = weighted_values(jnp.maximum(i - 1, 0), 1 - slot)
        _, alphas_new = softmax_update(ms, True, slot)
        accumulate(alphas, pvs)
        accumulate(alphas_new, weighted_values(i, slot))

    ss0 = scores(0)
    for hh in hds:
        s_ref[hh] = ss0[hh]
        p_ref[1, hh] = jnp.zeros((t, t), BF16)
        acc_ref[hh] = jnp.zeros((V_ROWS, t), F32)
    init = ([jnp.full((1, t), -jnp.inf, F32) for _ in hds], [jnp.ones((1, t), F32) for _ in hds],
            block_max(ss0))
    carry = lax.fori_loop(0, i // 2, lambda jj, cr: step(2 * jj + 1, 1, step(2 * jj, 0, cr)), init)

    @pl.when(i % 2 == 0)
    def _():
        diagonal(0, carry)

    @pl.when(i % 2 == 1)
    def _():
        diagonal(1, step(i - 1, 0, carry))

    o_t = jnp.concatenate([acc_ref[hh][:B_DH, :] / acc_ref[hh][B_DH:B_DH + 1, :] for hh in hds],
                          axis=0)
    o_ref[0] = (o_t.T * _silu(z_ref[0].astype(F32))).astype(BF16)


def _fox_attention(q_aug, k_aug, vt, z):
    bsz, seq, _ = z.shape
    t = ATT_T
    n_pairs = B_HEADS // HEADS_PER_STEP
    pw = HEADS_PER_STEP * AUG_W
    return pl.pallas_call(
        _attn_kernel,
        grid=(bsz, n_pairs, seq // t),
        in_specs=[pl.BlockSpec((1, t, pw), lambda b, p, i: (b, i, p)),
                  pl.BlockSpec((1, seq, pw), lambda b, p, i: (b, 0, p)),
                  pl.BlockSpec((1, HEADS_PER_STEP, seq // t, V_ROWS, t),
                               lambda b, p, i: (b, p, 0, 0, 0)),
                  pl.BlockSpec((1, t, HEADS_PER_STEP * B_DH), lambda b, p, i: (b, i, p))],
        out_specs=pl.BlockSpec((1, t, HEADS_PER_STEP * B_DH), lambda b, p, i: (b, i, p)),
        out_shape=jax.ShapeDtypeStruct((bsz, seq, B_W), BF16),
        scratch_shapes=[pltpu.VMEM((HEADS_PER_STEP, t, t), F32),
                        pltpu.VMEM((2, HEADS_PER_STEP, t, t), BF16),
                        pltpu.VMEM((HEADS_PER_STEP, V_ROWS, t), F32)],
        compiler_params=_cparams(("parallel", "parallel", "arbitrary")),
        name="fox_attention",
    )(q_aug, k_aug, vt, z)


def _pad_cols(w, n):
    return jnp.pad(w, ((0, 0), (0, n - w.shape[1])))


def _aug_weight(w, scale, width=AUG_W):
    d = w.shape[0]
    w = (w * scale).reshape(d, B_HEADS, B_DH)
    w = jnp.pad(w, ((0, 0), (0, 0), (0, width - B_DH)))
    return w.reshape(d, B_HEADS * width).astype(BF16)


def _placement(f_off, one_off, sign):
    place = np.zeros((LANES, B_HEADS * AUG_W), np.float32)
    ones = np.zeros((1, B_HEADS * AUG_W), np.float32)
    for p in range(N_FPARTS):
        for h in range(B_HEADS):
            place[p * B_HEADS + h, h * AUG_W + f_off + p] = sign
            ones[0, h * AUG_W + one_off + p] = 1.0
    return jnp.asarray(place, BF16), jnp.asarray(ones, F32)


def _vec3(m, n):
    return [t[:, None, :] for t in jnp.split(m, n, axis=-1)]


def kernel(x, c, a_mod_w, a_mod_b, a_pre_g, a_post_g, a_w_in, a_conv_w, a_a_log, a_dt_bias, a_o_gain, a_w_out, kv_mod_w, kv_mod_b, kv_norm_g, kv_w, kv_b_f, b_mod_w, b_mod_b, b_pre_g, b_post_g, b_w_in, b_w_out):
    place_q, ones_q = _placement(AUG_F0, AUG_F0 + N_FPARTS, 1.0)
    place_k, ones_k = _placement(AUG_F0 + N_FPARTS, AUG_F0, -1.0)

    for j in range(N_A_LAYERS):
        shift, scale, gate = _vec3(_adaln(c, a_mod_w[j], a_mod_b[j]), 3)
        w_in = a_w_in[j]
        n_main = A_CONV_CH + A_V
        w_main = w_in[:, :n_main].astype(BF16)
        w_ab = _pad_cols(w_in[:, n_main:], LANES).astype(BF16)
        qkv, z, ab = _a_in_proj(x, shift, scale, a_pre_g[j][None, :], w_main, w_ab)
        a_log_pad = _pad_cols(a_a_log[j][None, :], LANES)
        dt_pad = _pad_cols(a_dt_bias[j][None, :], LANES)
        og = _gdn_mixer(qkv, z, ab, a_conv_w[j], a_log_pad, dt_pad, a_o_gain[j][None, :])
        x = _out_proj(og, x, gate, a_post_g[j][None, :], a_w_out[j].astype(BF16))

    shift, scale = _vec3(_adaln(c, kv_mod_w, kv_mod_b), 2)
    wk_aug = _aug_weight(kv_w[:, :B_W], 1.0)
    wvt = _aug_weight(kv_w[:, B_W:2 * B_W], 1.0, V_ROWS).T
    wf = _pad_cols(kv_w[:, 2 * B_W:], LANES).astype(BF16)
    bf = _pad_cols(kv_b_f[None, :], LANES)
    k_aug, vt, fcat = _kv_proj(x, shift, scale, kv_norm_g[None, :], wk_aug, wvt, wf, bf, place_k, ones_k)

    for j in range(N_B_LAYERS):
        shift, scale, gate = _vec3(_adaln(c, b_mod_w[j], b_mod_b[j]), 3)
        wq_aug = _aug_weight(b_w_in[j][:, :B_W], B_DH ** -0.5)
        wz = b_w_in[j][:, B_W:].astype(BF16)
        q_aug, z = _b_in_proj(x, shift, scale, b_pre_g[j][None, :], wq_aug, wz, fcat, place_q, ones_q)
        og = _fox_attention(q_aug, k_aug, vt, z)
        x = _out_proj(og, x, gate, b_post_g[j][None, :], b_w_out[j].astype(BF16))
    return x
```
